```python
import jax, jax.numpy as jnp
from jax import lax
import numpy as np

D_MODEL = 2048
BATCH = 4
SEQ = 8192
DEPTH = 1
DEC_BATCH = 8
DEC_SEQ = 64
PAST_LEN = 2048

CHUNK = 64
MIX_WIDTH = D_MODEL
ATTN_WIDTH = MIX_WIDTH // 2
N_HEADS = 8
HEAD_DIM = ATTN_WIDTH // N_HEADS
IDX_HEADS = 16
IDX_DIM = 64
TOPK_MAX = 256
POOL_WIDTH = MIX_WIDTH - ATTN_WIDTH
POOL_WINDOWS = (2, 4, 8, 16)
POOL_GROUPS = len(POOL_WINDOWS)
POOL_GROUP_DIM = POOL_WIDTH // POOL_GROUPS
POOL_PAD = max(POOL_WINDOWS) - 1
D_FF = ((8 * D_MODEL + 3 * 256 - 1) // (3 * 256)) * 256
IN_SIZES = (ATTN_WIDTH, ATTN_WIDTH, ATTN_WIDTH, IDX_HEADS * IDX_DIM, IDX_DIM, IDX_HEADS, POOL_WIDTH)
IN_WIDTH = sum(IN_SIZES)
IN_SPLITS = [int(s) for s in np.cumsum(IN_SIZES)[:-1]]
Q_BLOCK = 128
NEG_INF = -1e30
EPS = 1e-6

kernel_name = "hymba_dsa_pool_stream_step"


def rmsnorm(x, g):
    x32 = x.astype(jnp.float32)
    y = x32 * lax.rsqrt(jnp.mean(x32 * x32, axis=-1, keepdims=True) + EPS) * g.astype(jnp.float32)
    return y.astype(x.dtype)


def ada_modulation(c, w_ada_l, b_ada_l):
    mod = jax.nn.silu(c) @ w_ada_l + b_ada_l
    return [m[:, None, :] for m in jnp.split(mod, 6, axis=-1)]


def split_proj(h, w_in_l):
    B, T, _ = h.shape
    p = jnp.einsum('btd,de->bte', h, w_in_l)
    q, k, v, qi, ki, wi, u = jnp.split(p, IN_SPLITS, axis=-1)
    q = q.reshape(B, T, N_HEADS, HEAD_DIM)
    k = k.reshape(B, T, N_HEADS, HEAD_DIM)
    v = v.reshape(B, T, N_HEADS, HEAD_DIM)
    qi = qi.reshape(B, T, IDX_HEADS, IDX_DIM)
    return q, k, v, qi, ki, wi, u


def dsa_attend(q, qi, wi, qpos, K, V, KI, kpos, topk):
    B, Q = q.shape[:2]
    f32 = jnp.float32
    dots = jnp.einsum('bqhd,bsd->bqhs', qi.astype(f32), KI.astype(f32)) * (IDX_DIM ** -0.5)
    score = jnp.einsum('bqh,bqhs->bqs', wi.astype(f32) * (IDX_HEADS ** -0.5), jax.nn.relu(dots))
    admissible = (kpos[None, :] // CHUNK) <= (qpos[:, None] // CHUNK)
    score = jnp.where(admissible[None], score, NEG_INF)
    top_val, top_idx = lax.top_k(score, topk)
    valid = top_val > 0.5 * NEG_INF
    Kg = jax.vmap(lambda kb, ib: kb[ib])(K, top_idx)
    Vg = jax.vmap(lambda vb, ib: vb[ib])(V, top_idx)
    logits = jnp.einsum('bqhd,bqkhd->bqhk', q.astype(f32), Kg.astype(f32)) * (HEAD_DIM ** -0.5)
    logits = jnp.where(valid[:, :, None, :], logits, NEG_INF)
    p = jax.nn.softmax(logits, axis=-1)
    o = jnp.einsum('bqhk,bqkhd->bqhd', p.astype(Vg.dtype), Vg)
    return o.reshape(B, Q, ATTN_WIDTH)


def dsa_prompt(q, k, v, qi, ki, wi, pos, topk):
    B, T = q.shape[:2]
    nb = T // Q_BLOCK

    def blockify(a):
        return jnp.moveaxis(a.reshape(a.shape[0], nb, Q_BLOCK, *a.shape[2:]), 1, 0)

    xs = (blockify(q), blockify(qi), blockify(wi), pos.reshape(nb, Q_BLOCK))
    out = lax.map(lambda blk: dsa_attend(blk[0], blk[1], blk[2], blk[3], k, v, ki, pos, topk), xs)
    return jnp.moveaxis(out, 0, 1).reshape(B, T, ATTN_WIDTH)


def pool_mixer(u, prefix, pos, w_pool_l, scale_l):
    B, T, _ = u.shape
    f32 = jnp.float32
    xp = jnp.concatenate([prefix, u], axis=1).astype(f32)
    cs = jnp.concatenate([jnp.zeros((B, 1, POOL_WIDTH), f32), jnp.cumsum(xp, axis=1)], axis=1)
    end = cs[:, POOL_PAD + 1:POOL_PAD + 1 + T]
    outs = []
    for g, w in enumerate(POOL_WINDOWS):
        sl = slice(g * POOL_GROUP_DIM, (g + 1) * POOL_GROUP_DIM)
        start = cs[:, POOL_PAD + 1 - w:POOL_PAD + 1 - w + T, sl]
        cnt = jnp.minimum(pos + 1, w).astype(f32)[None, :, None]
        outs.append((end[..., sl] - start) / cnt - u[..., sl].astype(f32))
    z = jnp.stack(outs, axis=2)
    y = jnp.einsum('btgc,gce->btge', z, w_pool_l.astype(f32)).reshape(B, T, POOL_WIDTH)
    return (y * scale_l.astype(f32)).astype(u.dtype)


def residual_update(x, attn_o, pool_o, mods, w_out_l, g2, w_gate_l, w_up_l, w_down_l):
    _, _, gate1, shift2, scale2, gate2 = mods
    mix = jnp.einsum('btm,md->btd', jnp.concatenate([attn_o, pool_o], axis=-1), w_out_l)
    x = x + gate1 * mix
    h2 = rmsnorm(x, g2) * (1.0 + scale2) + shift2
    ff = jax.nn.silu(h2 @ w_gate_l) * (h2 @ w_up_l)
    return x + gate2 * (ff @ w_down_l)


def setup_inputs(seed: int = 0) -> dict:
    key = jax.random.key(seed)
    ks = jax.random.split(key, 24)
    f32 = jnp.float32
    nrm = lambda k, shape, s: (jax.random.normal(k, shape, f32) * s)
    return {
        "x_prompt": nrm(ks[0], (BATCH, SEQ, D_MODEL), 1.0),
        "x_sample": nrm(ks[1], (DEC_BATCH, DEC_SEQ, D_MODEL), 1.0),
        "c_prompt": nrm(ks[2], (BATCH, D_MODEL), 1.0),
        "c_sample": nrm(ks[3], (DEC_BATCH, D_MODEL), 1.0),
        "cache_k": nrm(ks[4], (DEPTH, DEC_BATCH, PAST_LEN, N_HEADS, HEAD_DIM), 1.0),
        "cache_v": nrm(ks[5], (DEPTH, DEC_BATCH, PAST_LEN, N_HEADS, HEAD_DIM), 1.0),
        "cache_kidx": nrm(ks[6], (DEPTH, DEC_BATCH, PAST_LEN, IDX_DIM), 1.0),
        "state_pool": nrm(ks[7], (DEPTH, DEC_BATCH, POOL_PAD, POOL_WIDTH), 1.0),
        "w_ada": nrm(ks[8], (DEPTH, D_MODEL, 6 * D_MODEL), 0.3 * D_MODEL ** -0.5),
        "b_ada": nrm(ks[9], (DEPTH, 6 * D_MODEL), 0.02),
        "g_norm1": 1.0 + nrm(ks[10], (DEPTH, D_MODEL), 0.02),
        "w_in": nrm(ks[11], (DEPTH, D_MODEL, IN_WIDTH), D_MODEL ** -0.5),
        "w_pool": nrm(ks[12], (DEPTH, POOL_GROUPS, POOL_GROUP_DIM, POOL_GROUP_DIM), POOL_GROUP_DIM ** -0.5),
        "pool_scale": 1.0 + nrm(ks[13], (DEPTH, POOL_WIDTH), 0.02),
        "w_out": nrm(ks[14], (DEPTH, MIX_WIDTH, D_MODEL), MIX_WIDTH ** -0.5),
        "g_norm2": 1.0 + nrm(ks[15], (DEPTH, D_MODEL), 0.02),
        "w_gate": nrm(ks[16], (DEPTH, D_MODEL, D_FF), D_MODEL ** -0.5),
        "w_up": nrm(ks[17], (DEPTH, D_MODEL, D_FF), D_MODEL ** -0.5),
        "w_down": nrm(ks[18], (DEPTH, D_FF, D_MODEL), D_FF ** -0.5),
        "g_final": 1.0 + nrm(ks[19], (D_MODEL,), 0.02),
    }


def reference(x_prompt, x_sample, c_prompt, c_sample, cache_k, cache_v, cache_kidx, state_pool,
              w_ada, b_ada, g_norm1, w_in, w_pool, pool_scale, w_out, g_norm2, w_gate, w_up, w_down, g_final):
    B, T = x_prompt.shape[:2]
    Bs, Ts = x_sample.shape[:2]
    P = cache_k.shape[2]
    topk_p = min(TOPK_MAX, T // 4)
    topk_s = min(TOPK_MAX, (P + Ts) // 4)
    pos_p = jnp.arange(T, dtype=jnp.int32)
    pos_s = P + jnp.arange(Ts, dtype=jnp.int32)
    kpos_s = jnp.arange(P + Ts, dtype=jnp.int32)

    xp, xs = x_prompt, x_sample
    kp_l, vp_l, kip_l, pp_l = [], [], [], []
    ks_l, vs_l, kis_l, ps_l = [], [], [], []
    for l in range(DEPTH):
        mods = ada_modulation(c_prompt, w_ada[l], b_ada[l])
        h = rmsnorm(xp, g_norm1[l]) * (1.0 + mods[1]) + mods[0]
        q, k, v, qi, ki, wi, u = split_proj(h, w_in[l])
        attn_o = dsa_prompt(q, k, v, qi, ki, wi, pos_p, topk_p)
        prefix = jnp.zeros((B, POOL_PAD, POOL_WIDTH), u.dtype)
        pool_o = pool_mixer(u, prefix, pos_p, w_pool[l], pool_scale[l])
        xp = residual_update(xp, attn_o, pool_o, mods, w_out[l], g_norm2[l], w_gate[l], w_up[l], w_down[l])
        kp_l.append(k)
        vp_l.append(v)
        kip_l.append(ki)
        pp_l.append(jnp.concatenate([prefix, u], axis=1)[:, -POOL_PAD:])

        mods_s = ada_modulation(c_sample, w_ada[l], b_ada[l])
        hs = rmsnorm(xs, g_norm1[l]) * (1.0 + mods_s[1]) + mods_s[0]
        q_s, k_s, v_s, qi_s, ki_s, wi_s, u_s = split_proj(hs, w_in[l])
        K_all = jnp.concatenate([cache_k[l], k_s], axis=1)
        V_all = jnp.concatenate([cache_v[l], v_s], axis=1)
        KI_all = jnp.concatenate([cache_kidx[l], ki_s], axis=1)
        attn_s = dsa_attend(q_s, qi_s, wi_s, pos_s, K_all, V_all, KI_all, kpos_s, topk_s)
        pool_s = pool_mixer(u_s, state_pool[l], pos_s, w_pool[l], pool_scale[l])
        xs = residual_update(xs, attn_s, pool_s, mods_s, w_out[l], g_norm2[l], w_gate[l], w_up[l], w_down[l])
        ks_l.append(k_s)
        vs_l.append(v_s)
        kis_l.append(ki_s)
        ps_l.append(jnp.concatenate([state_pool[l], u_s], axis=1)[:, -POOL_PAD:])

    y_prompt = rmsnorm(xp, g_final)
    y_sample = rmsnorm(xs, g_final)
    return (y_prompt, y_sample,
            jnp.stack(kp_l), jnp.stack(vp_l), jnp.stack(kip_l), jnp.stack(pp_l),
            jnp.stack(ks_l), jnp.stack(vs_l), jnp.stack(kis_l), jnp.stack(ps_l))
```

```python
import functools

import numpy as np
import jax
import jax.numpy as jnp
from jax import lax
from jax.experimental import pallas as pl
from jax.experimental.pallas import tpu as pltpu

F32 = jnp.float32
BF16 = jnp.bfloat16

CHUNK = 64
N_HEADS = 8
HEAD_DIM = 128
ATTN_WIDTH = N_HEADS * HEAD_DIM
IDX_HEADS = 16
IDX_DIM = 64
TOPK_MAX = 256
POOL_WINDOWS = (2, 4, 8, 16)
POOL_GROUP_DIM = 256
POOL_WIDTH = POOL_GROUP_DIM * len(POOL_WINDOWS)
POOL_PAD = max(POOL_WINDOWS) - 1
HALO = 16
NEG_INF = -1e30
EPS = 1e-6
VALID_MIN = float(np.nextafter(np.float32(0.5) * np.float32(NEG_INF), np.float32(0.0)))
MISC_WIDTH = 128
V7X_VMEM_BYTES = 64 * 1024 * 1024
VMEM_LIMIT = V7X_VMEM_BYTES - 8 * 1024 * 1024


def _params(sem):
    return pltpu.CompilerParams(dimension_semantics=sem, vmem_limit_bytes=VMEM_LIMIT)


def _resident(block_shape, index_map):
    return pl.BlockSpec(block_shape, index_map, pipeline_mode=pl.Buffered(1))


def _mod_spec(mod, tm):
    d = mod.shape[-1]
    if mod.shape[1] == 1:
        return pl.BlockSpec((1, 1, d), lambda b, i, *_: (b, 0, 0))
    return pl.BlockSpec((1, tm, d), lambda b, i, *_: (b, i, 0))


def _ada_kernel(c_ref, w_ref, b_ref, o_ref):
    c = c_ref[...]
    a = (c * jax.nn.sigmoid(c)).astype(BF16)
    o_ref[...] = jnp.dot(a, w_ref[...].astype(BF16), preferred_element_type=F32) + b_ref[...]


def _ada(c, w_ada, b_ada):
    rows, d = c.shape
    n = w_ada.shape[1]
    tn = 1536
    return pl.pallas_call(
        _ada_kernel,
        grid=(n // tn,),
        in_specs=[pl.BlockSpec((rows, d), lambda j: (0, 0)),
                  pl.BlockSpec((d, tn), lambda j: (0, j)),
                  pl.BlockSpec((1, tn), lambda j: (0, j))],
        out_specs=pl.BlockSpec((rows, tn), lambda j: (0, j)),
        out_shape=jax.ShapeDtypeStruct((rows, n), F32),
        compiler_params=_params(("arbitrary",)),
        name="ada",
    )(c, w_ada, b_ada.reshape(1, n))


_Q0, _K0, _V0, _QI0, _U0, _MISC0, _IN_PACKED = 0, 1024, 2048, 3072, 4096, 5120, 5248


def _rmsnorm_mod(x, g, scale, shift):
    y = x * lax.rsqrt(jnp.mean(x * x, axis=-1, keepdims=True) + EPS) * g
    return y * (1.0 + scale) + shift


def _inproj_kernel(x_ref, shift_ref, scale_ref, g_ref, w_ref,
                   q_ref, k_ref, v_ref, qi_ref, u_ref, misc_ref, kb_ref, vb_ref):
    h = _rmsnorm_mod(x_ref[0], g_ref[...], scale_ref[0], shift_ref[0]).astype(BF16)

    def proj(c0, width):
        return jnp.dot(h, w_ref[:, c0:c0 + width], preferred_element_type=F32)

    q_ref[0] = proj(_Q0, ATTN_WIDTH).astype(BF16)
    k = proj(_K0, ATTN_WIDTH)
    k_ref[0] = k
    kb_ref[0] = k.astype(BF16)
    v = proj(_V0, ATTN_WIDTH)
    v_ref[0] = v
    vb_ref[0] = v.astype(BF16)
    qi = proj(_QI0, IDX_HEADS * IDX_DIM).astype(BF16)
    for hh in range(IDX_HEADS):
        qi_ref[0, hh] = qi[:, hh * IDX_DIM:(hh + 1) * IDX_DIM]
    u_ref[0] = proj(_U0, POOL_WIDTH)
    misc_ref[0] = proj(_MISC0, MISC_WIDTH)


def _inproj(x, shift1, scale1, g1, w_packed, tm):
    bsz, t, d = x.shape
    tok = lambda width, dt: (jax.ShapeDtypeStruct((bsz, t, width), dt),
                             pl.BlockSpec((1, tm, width), lambda b, i: (b, i, 0)))
    outs = [tok(ATTN_WIDTH, BF16), tok(ATTN_WIDTH, F32), tok(ATTN_WIDTH, F32),
            (jax.ShapeDtypeStruct((bsz, IDX_HEADS, t, IDX_DIM), BF16),
             pl.BlockSpec((1, IDX_HEADS, tm, IDX_DIM), lambda b, i: (b, 0, i, 0))),
            tok(POOL_WIDTH, F32), tok(MISC_WIDTH, F32), tok(ATTN_WIDTH, BF16), tok(ATTN_WIDTH, BF16)]
    return pl.pallas_call(
        _inproj_kernel,
        grid=(bsz, t // tm),
        in_specs=[pl.BlockSpec((1, tm, d), lambda b, i: (b, i, 0)),
                  _mod_spec(shift1, tm), _mod_spec(scale1, tm),
                  pl.BlockSpec((1, d), lambda b, i: (0, 0)),
                  _resident((d, _IN_PACKED), lambda b, i: (0, 0))],
        out_specs=[o[1] for o in outs],
        out_shape=[o[0] for o in outs],
        compiler_params=_params(("arbitrary", "arbitrary")),
        name="inproj",
    )(x, shift1, scale1, g1.reshape(1, d), w_packed)


def _key_to_float(u):
    bits = jnp.where(u < 0, u & jnp.int32(0x7FFFFFFF), ~u)
    return lax.bitcast_convert_type(bits, F32)


def _dsa_kernel(q_ref, qi_ref, misc_ref, kit_ref, k_ref, v_ref, o_ref,
                sc_ref, m_ref, l_ref, acc_ref, *, tq, ts, tk, pos0, topk):
    i = pl.program_id(1)
    s_pad = sc_ref.shape[1]
    lim = jnp.minimum(((pos0 + (i + 1) * tq - 1) // CHUNK + 1) * CHUNK, s_pad)
    n_s = (lim + ts - 1) // ts
    n_k = (lim + tk - 1) // tk
    qchunk = (pos0 + i * tq + lax.broadcasted_iota(jnp.int32, (tq, 1), 0)) // CHUNK
    wv = misc_ref[0][:, IDX_DIM:IDX_DIM + IDX_HEADS] * (IDX_HEADS ** -0.5 * IDX_DIM ** -0.5)

    def score_tile(t, carry):
        s0 = pl.multiple_of(t * ts, ts)
        kt = kit_ref[0, :, pl.ds(s0, ts)]
        acc = jnp.zeros((tq, ts), F32)
        for hh in range(IDX_HEADS):
            d = jnp.dot(qi_ref[0, hh], kt, preferred_element_type=F32)
            acc = acc + jnp.maximum(d, 0.0) * wv[:, hh:hh + 1]
        kchunk = (s0 + lax.broadcasted_iota(jnp.int32, (1, ts), 1)) // CHUNK
        sc_ref[:, pl.ds(s0, ts)] = jnp.where(kchunk <= qchunk, acc, NEG_INF)
        return carry

    lax.fori_loop(0, n_s, score_tile, 0)

    def count(pred):
        def body(t, c):
            s0 = pl.multiple_of(t * ts, ts)
            x = sc_ref[:, pl.ds(s0, ts)]
            for cc in range(ts // 128):
                c = c + jnp.where(pred(x[:, cc * 128:(cc + 1) * 128], s0 + cc * 128), 1.0, 0.0)
            return c
        c = lax.fori_loop(0, n_s, body, jnp.zeros((tq, 128), F32))
        return jnp.sum(c, axis=-1, keepdims=True)

    def bit_body(b, carry):
        u, cnt_u = carry
        trial = u | lax.shift_left(jnp.int32(1), 31 - b)
        f = jnp.broadcast_to(_key_to_float(trial), (tq, 128))
        c = count(lambda x, _: x >= f)
        ok = c >= topk
        return jnp.where(ok, trial, u), jnp.where(ok, c, cnt_u)

    u, cnt_ge = lax.fori_loop(0, 32, bit_body,
                              (jnp.zeros((tq, 1), jnp.int32), jnp.zeros((tq, 1), F32)))
    tau = _key_to_float(u)
    tau_eff = jnp.maximum(tau, VALID_MIN)
    tie_row = jnp.logical_and(cnt_ge > topk, tau >= VALID_MIN)
    any_tie = jnp.max(jnp.where(tie_row, 1.0, 0.0)) > 0.0

    def write_bias(sel_fn):
        def body(t, carry):
            s0 = pl.multiple_of(t * ts, ts)
            x = sc_ref[:, pl.ds(s0, ts)]
            sc_ref[:, pl.ds(s0, ts)] = jnp.where(sel_fn(x, s0), 0.0, NEG_INF)
            return carry
        lax.fori_loop(0, n_s, body, 0)

    @pl.when(jnp.logical_not(any_tie))
    def _():
        write_bias(lambda x, s0: x >= tau_eff)

    @pl.when(any_tie)
    def _():
        taub = jnp.broadcast_to(tau, (tq, 128))
        need = topk - count(lambda x, _: x > taub)
        lane = lax.broadcasted_iota(jnp.int32, (1, 128), 1)

        def idx_body(b, lo):
            trial = lo | lax.shift_left(jnp.int32(1), (s_pad.bit_length() - 1) - b)
            c = count(lambda x, c0: jnp.logical_and(x == taub, (c0 + lane) < trial))
            return jnp.where(c < need, trial, lo)

        lo = lax.fori_loop(0, s_pad.bit_length(), idx_body, jnp.zeros((tq, 1), jnp.int32))
        jstar = jnp.where(tie_row, lo + 1, s_pad)

        def sel(x, s0):
            idx = s0 + lax.broadcasted_iota(jnp.int32, (1, ts), 1)
            keep = jnp.logical_or(x > tau, jnp.logical_and(x == tau, idx < jstar))
            return jnp.logical_and(keep, x >= VALID_MIN)
        write_bias(sel)

    m_ref[...] = jnp.full(m_ref.shape, NEG_INF, F32)
    l_ref[...] = jnp.zeros(l_ref.shape, F32)
    acc_ref[...] = jnp.zeros(acc_ref.shape, F32)
    scale = HEAD_DIM ** -0.5

    def kv_body(t, carry):
        k0 = pl.multiple_of(t * tk, tk)
        bias = sc_ref[:, pl.ds(k0, tk)]
        for hh in range(N_HEADS):
            hs = slice(hh * HEAD_DIM, (hh + 1) * HEAD_DIM)
            s = lax.dot_general(q_ref[0, :, hs], k_ref[0, pl.ds(k0, tk), hs],
                                (((1,), (1,)), ((), ())), preferred_element_type=F32) * scale + bias
            m_prev = m_ref[hh]
            m_new = jnp.maximum(m_prev, jnp.max(s, axis=-1, keepdims=True))
            alpha = jnp.exp(m_prev - m_new)
            p = jnp.exp(s - m_new)
            l_ref[hh] = alpha * l_ref[hh] + jnp.sum(p, axis=-1, keepdims=True)
            acc_ref[:, hs] = alpha * acc_ref[:, hs] + jnp.dot(
                p.astype(BF16), v_ref[0, pl.ds(k0, tk), hs], preferred_element_type=F32)
            m_ref[hh] = m_new
        return carry

    lax.fori_loop(0, n_k, kv_body, 0)
    for hh in range(N_HEADS):
        hs = slice(hh * HEAD_DIM, (hh + 1) * HEAD_DIM)
        o_ref[0, :, hs] = (acc_ref[:, hs] / l_ref[hh]).astype(o_ref.dtype)


def _dsa(q, qi, misc_q, kit, kb, vb, *, pos0, topk, tq, ts, tk):
    bsz, t, _ = q.shape
    s_pad = kb.shape[1]
    assert s_pad % ts == 0 and s_pad % tk == 0 and ts >= topk and t % tq == 0
    kern = functools.partial(_dsa_kernel, tq=tq, ts=ts, tk=tk, pos0=pos0, topk=topk)
    return pl.pallas_call(
        kern,
        grid=(bsz, t // tq),
        in_specs=[pl.BlockSpec((1, tq, ATTN_WIDTH), lambda b, i: (b, i, 0)),
                  pl.BlockSpec((1, IDX_HEADS, tq, IDX_DIM), lambda b, i: (b, 0, i, 0)),
                  pl.BlockSpec((1, tq, MISC_WIDTH), lambda b, i: (b, i, 0)),
                  _resident((1, IDX_DIM, s_pad), lambda b, i: (b, 0, 0)),
                  _resident((1, s_pad, ATTN_WIDTH), lambda b, i: (b, 0, 0)),
                  _resident((1, s_pad, ATTN_WIDTH), lambda b, i: (b, 0, 0))],
        out_specs=pl.BlockSpec((1, tq, ATTN_WIDTH), lambda b, i: (b, i, 0)),
        out_shape=jax.ShapeDtypeStruct((bsz, t, ATTN_WIDTH), BF16),
        scratch_shapes=[pltpu.VMEM((tq, s_pad), F32),
                        pltpu.VMEM((N_HEADS, tq, 1), F32),
                        pltpu.VMEM((N_HEADS, tq, 1), F32),
                        pltpu.VMEM((tq, ATTN_WIDTH), F32)],
        compiler_params=_params(("arbitrary", "arbitrary")),
        name="dsa",
    )(q, qi, misc_q, kit, kb, vb)


def _mix_kernel(x_ref, attn_ref, u_ref, halo_ref, gate1_ref, shift2_ref, scale2_ref, g2_ref,
                wpool_ref, pscale_ref, wout_ref, x1_ref, h2_ref, xp_ref, *, tm, pos0, zero_first_halo):
    i = pl.program_id(1)
    halo = halo_ref[0]
    if zero_first_halo:
        halo = jnp.where(i == 0, 0.0, halo)
    xp_ref[0:HALO] = halo
    xp_ref[HALO:HALO + tm] = u_ref[0]
    pos = pos0 + i * tm + lax.broadcasted_iota(jnp.int32, (tm, 1), 0)
    mix = jnp.dot(attn_ref[0], wout_ref[0:ATTN_WIDTH], preferred_element_type=F32)
    for g, w in enumerate(POOL_WINDOWS):
        sl = slice(g * POOL_GROUP_DIM, (g + 1) * POOL_GROUP_DIM)
        cur = xp_ref[HALO:HALO + tm, sl]
        wsum = cur
        for dlt in range(1, w):
            wsum = wsum + xp_ref[HALO - dlt:HALO - dlt + tm, sl]
        cnt = jnp.minimum(pos + 1, w).astype(F32)
        z = wsum / cnt - cur
        y = jnp.dot(z.astype(BF16), wpool_ref[g], preferred_element_type=F32) * pscale_ref[:, sl]
        mix = mix + jnp.dot(y.astype(BF16), wout_ref[ATTN_WIDTH + g * POOL_GROUP_DIM:
                                                      ATTN_WIDTH + (g + 1) * POOL_GROUP_DIM],
                            preferred_element_type=F32)
    x1 = x_ref[0] + gate1_ref[0] * mix
    x1_ref[0] = x1
    h2_ref[0] = _rmsnorm_mod(x1, g2_ref[...], scale2_ref[0], shift2_ref[0]).astype(BF16)


def _mix(x, attn, u, halo, gate1, shift2, scale2, g2, wpool, pscale, wout, *, tm, pos0, halo_from_u):
    bsz, t, d = x.shape
    if halo_from_u:
        halo_spec = pl.BlockSpec((1, HALO, POOL_WIDTH),
                                 lambda b, i: (b, jnp.maximum(i * (tm // HALO) - 1, 0), 0))
    else:
        halo_spec = pl.BlockSpec((1, HALO, POOL_WIDTH), lambda b, i: (b, 0, 0))
    kern = functools.partial(_mix_kernel, tm=tm, pos0=pos0, zero_first_halo=halo_from_u)
    tokspec = lambda width: pl.BlockSpec((1, tm, width), lambda b, i: (b, i, 0))
    return pl.pallas_call(
        kern,
        grid=(bsz, t // tm),
        in_specs=[tokspec(d), tokspec(ATTN_WIDTH), tokspec(POOL_WIDTH), halo_spec,
                  _mod_spec(gate1, tm), _mod_spec(shift2, tm), _mod_spec(scale2, tm),
                  pl.BlockSpec((1, d), lambda b, i: (0, 0)),
                  _resident(wpool.shape, lambda b, i: (0, 0, 0)),
                  pl.BlockSpec((1, POOL_WIDTH), lambda b, i: (0, 0)),
                  _resident(wout.shape, lambda b, i: (0, 0))],
        out_specs=[tokspec(d), tokspec(d)],
        out_shape=[jax.ShapeDtypeStruct((bsz, t, d), F32), jax.ShapeDtypeStruct((bsz, t, d), BF16)],
        scratch_shapes=[pltpu.VMEM((HALO + tm, POOL_WIDTH), F32)],
        compiler_params=_params(("arbitrary", "arbitrary")),
        name="mix",
    )(x, attn, u, halo, gate1, shift2, scale2, g2.reshape(1, d), wpool, pscale.reshape(1, POOL_WIDTH), wout)


def _ffn_kernel(h_ref, x1_ref, gate2_ref, gf_ref, wg_ref, wu_ref, wd_ref, y_ref, acc_ref):
    j = pl.program_id(2)

    @pl.when(j == 0)
    def _():
        acc_ref[...] = jnp.zeros(acc_ref.shape, F32)

    h = h_ref[0]
    g = jnp.dot(h, wg_ref[...], preferred_element_type=F32)
    up = jnp.dot(h, wu_ref[...], preferred_element_type=F32)
    a = (g * jax.nn.sigmoid(g) * up).astype(BF16)
    acc_ref[...] += jnp.dot(a, wd_ref[...], preferred_element_type=F32)

    @pl.when(j == pl.num_programs(2) - 1)
    def _():
        out = x1_ref[0] + gate2_ref[0] * acc_ref[...]
        y_ref[0] = out * lax.rsqrt(jnp.mean(out * out, axis=-1, keepdims=True) + EPS) * gf_ref[...]


def _ffn(h2, x1, gate2, g_final, wg, wu, wd, *, tm, tf):
    bsz, t, d = x1.shape
    dff = wg.shape[1]
    tokspec = pl.BlockSpec((1, tm, d), lambda b, i, j: (b, i, 0))
    return pl.pallas_call(
        _ffn_kernel,
        grid=(bsz, t // tm, dff // tf),
        in_specs=[tokspec, tokspec, _mod_spec(gate2, tm),
                  pl.BlockSpec((1, d), lambda b, i, j: (0, 0)),
                  pl.BlockSpec((d, tf), lambda b, i, j: (0, j)),
                  pl.BlockSpec((d, tf), lambda b, i, j: (0, j)),
                  pl.BlockSpec((tf, d), lambda b, i, j: (j, 0))],
        out_specs=tokspec,
        out_shape=jax.ShapeDtypeStruct((bsz, t, d), F32),
        scratch_shapes=[pltpu.VMEM((tm, d), F32)],
        compiler_params=_params(("arbitrary", "arbitrary", "arbitrary")),
        name="ffn",
    )(h2, x1, gate2, g_final.reshape(1, d), wg, wu, wd)


def _pad_keys(a, s_pad, axis):
    pad = [(0, 0)] * a.ndim
    pad[axis] = (0, s_pad - a.shape[axis])
    return jnp.pad(a, pad)


def kernel(x_prompt, x_sample, c_prompt, c_sample, cache_k, cache_v, cache_kidx, state_pool,
           w_ada, b_ada, g_norm1, w_in, w_pool, pool_scale, w_out, g_norm2, w_gate, w_up, w_down, g_final):
    bp, tp, d = x_prompt.shape
    bs, tsmp, _ = x_sample.shape
    depth = w_ada.shape[0]
    past = cache_k.shape[2]
    assert depth == 1 and d == 2 * ATTN_WIDTH
    topk_p = min(TOPK_MAX, tp // 4)
    topk_s = min(TOPK_MAX, (past + tsmp) // 4)
    l = 0

    w = w_in[l]
    w_packed = jnp.concatenate(
        [w[:, :4096], w[:, 4176:5200], w[:, 4096:4176], jnp.zeros((d, _IN_PACKED - 5200), w.dtype)],
        axis=1).astype(BF16)
    wpool_b = w_pool[l].astype(BF16)
    wout_b = w_out[l].astype(BF16)
    wg_b, wu_b, wd_b = w_gate[l].astype(BF16), w_up[l].astype(BF16), w_down[l].astype(BF16)

    rows = bp + bs
    c_all = jnp.concatenate([c_prompt, c_sample], axis=0)
    c_all = jnp.pad(c_all, ((0, -rows % 8), (0, 0)))
    mod = _ada(c_all, w_ada[l], b_ada[l])
    mods_p = [m[:bp, None, :] for m in jnp.split(mod, 6, axis=-1)]
    mods_s = [jnp.repeat(m[bp:rows], tsmp, axis=0)[None] for m in jnp.split(mod, 6, axis=-1)]
    mods_sb = [m[bp:rows, None, :] for m in jnp.split(mod, 6, axis=-1)]

    q, k, v, qi, u, misc, kb, vb = _inproj(x_prompt, mods_p[0], mods_p[1], g_norm1[l], w_packed, tm=256)
    kit = jnp.swapaxes(misc[..., :IDX_DIM], 1, 2).astype(BF16)
    attn = _dsa(q, qi, misc, kit, kb, vb, pos0=0, topk=topk_p, tq=128, ts=512, tk=512)
    x1, h2 = _mix(x_prompt, attn, u, u, mods_p[2], mods_p[3], mods_p[4], g_norm2[l],
                  wpool_b, pool_scale[l], wout_b, tm=512, pos0=0, halo_from_u=True)
    y_prompt = _ffn(h2, x1, mods_p[5], g_final, wg_b, wu_b, wd_b, tm=512, tf=512)

    ntok = bs * tsmp
    xs_flat = x_sample.reshape(1, ntok, d)
    q_s, k_s, v_s, qi_s, u_s, misc_s, kb_s, vb_s = _inproj(
        xs_flat, mods_s[0], mods_s[1], g_norm1[l], w_packed, tm=min(256, ntok))
    per_batch = lambda a: a.reshape(bs, tsmp, a.shape[-1])
    q_s, k_s, v_s, u_s, misc_s, kb_s, vb_s = map(per_batch, (q_s, k_s, v_s, u_s, misc_s, kb_s, vb_s))
    qi_s = jnp.swapaxes(qi_s.reshape(IDX_HEADS, bs, tsmp, IDX_DIM), 0, 1)
    s_all = past + tsmp
    s_pad = -(-s_all // 512) * 512
    k_all = _pad_keys(jnp.concatenate([cache_k[l].reshape(bs, past, ATTN_WIDTH).astype(BF16), kb_s], axis=1), s_pad, 1)
    v_all = _pad_keys(jnp.concatenate([cache_v[l].reshape(bs, past, ATTN_WIDTH).astype(BF16), vb_s], axis=1), s_pad, 1)
    ki_all = jnp.concatenate([cache_kidx[l], misc_s[..., :IDX_DIM]], axis=1)
    kit_s = _pad_keys(jnp.swapaxes(ki_all, 1, 2).astype(BF16), s_pad, 2)
    attn_s = _dsa(q_s, qi_s, misc_s, kit_s, k_all, v_all, pos0=past, topk=topk_s, tq=tsmp, ts=512, tk=512)
    halo_s = jnp.pad(state_pool[l], ((0, 0), (HALO - POOL_PAD, 0), (0, 0)))
    x1_s, h2_s = _mix(x_sample, attn_s, u_s, halo_s, mods_sb[2], mods_sb[3], mods_sb[4], g_norm2[l],
                      wpool_b, pool_scale[l], wout_b, tm=tsmp, pos0=past, halo_from_u=False)
    y_sample = _ffn(h2_s.reshape(1, ntok, d), x1_s.reshape(1, ntok, d), mods_s[5], g_final,
                    wg_b, wu_b, wd_b, tm=min(256, ntok), tf=512).reshape(bs, tsmp, d)

    heads = lambda a: a.reshape(1, a.shape[0], a.shape[1], N_HEADS, HEAD_DIM)
    assert tp >= POOL_PAD and tsmp >= POOL_PAD
    pool_p = u[:, -POOL_PAD:]
    pool_s = u_s[:, -POOL_PAD:]
    return (y_prompt, y_sample,
            heads(k), heads(v), misc[..., :IDX_DIM][None], pool_p[None],
            heads(k_s), heads(v_s), misc_s[..., :IDX_DIM][None], pool_s[None])
```

```python
import functools

import numpy as np
import jax
import jax.numpy as jnp
from jax import lax
from jax.experimental import pallas as pl
from jax.experimental.pallas import tpu as pltpu

F32 = jnp.float32
BF16 = jnp.bfloat16

CHUNK = 64
N_HEADS = 8
HEAD_DIM = 128
ATTN_WIDTH = N_HEADS * HEAD_DIM
IDX_HEADS = 16
IDX_DIM = 64
TOPK_MAX = 256
POOL_WINDOWS = (2, 4, 8, 16)
POOL_GROUP_DIM = 256
POOL_WIDTH = POOL_GROUP_DIM * len(POOL_WINDOWS)
POOL_PAD = max(POOL_WINDOWS) - 1
HALO = 16
NEG_INF = -1e30
EPS = 1e-6
VALID_MIN = float(np.nextafter(np.float32(0.5) * np.float32(NEG_INF), np.float32(0.0)))
F32_MAX = float(np.finfo(np.float32).max)
MISC_WIDTH = 128
Q_PRESCALE = float(HEAD_DIM ** -0.5 * np.log2(np.e))
V_ROWS = HEAD_DIM + 16
SEARCH_MAX_ITERS = 24
V7X_VMEM_BYTES = 64 * 1024 * 1024
VMEM_LIMIT = V7X_VMEM_BYTES - 8 * 1024 * 1024


def _params(sem):
    return pltpu.CompilerParams(dimension_semantics=sem, vmem_limit_bytes=VMEM_LIMIT)


def _resident(block_shape, index_map):
    return pl.BlockSpec(block_shape, index_map, pipeline_mode=pl.Buffered(1))


def _mod_spec(mod, tm):
    d = mod.shape[-1]
    if mod.shape[1] == 1:
        return pl.BlockSpec((1, 1, d), lambda b, i, *_: (b, 0, 0))
    return pl.BlockSpec((1, tm, d), lambda b, i, *_: (b, i, 0))


def _ada_kernel(c_ref, w_ref, b_ref, o_ref):
    c = c_ref[...]
    a = (c * jax.nn.sigmoid(c)).astype(BF16)
    o_ref[...] = jnp.dot(a, w_ref[...].astype(BF16), preferred_element_type=F32) + b_ref[...]


def _ada(c, w_ada, b_ada):
    rows, d = c.shape
    n = w_ada.shape[1]
    tn = 1536
    return pl.pallas_call(
        _ada_kernel,
        grid=(n // tn,),
        in_specs=[pl.BlockSpec((rows, d), lambda j: (0, 0)),
                  pl.BlockSpec((d, tn), lambda j: (0, j)),
                  pl.BlockSpec((1, tn), lambda j: (0, j))],
        out_specs=pl.BlockSpec((rows, tn), lambda j: (0, j)),
        out_shape=jax.ShapeDtypeStruct((rows, n), F32),
        compiler_params=_params(("arbitrary",)),
        name="ada",
    )(c, w_ada, b_ada.reshape(1, n))


_Q0, _K0, _V0, _QI0, _U0, _MISC0, _IN_PACKED = 0, 1024, 2048, 3072, 4096, 5120, 5248


def _rmsnorm_mod(x, g, scale, shift):
    y = x * lax.rsqrt(jnp.mean(x * x, axis=-1, keepdims=True) + EPS) * g
    return y * (1.0 + scale) + shift


def _inproj_kernel(x_ref, shift_ref, scale_ref, g_ref, w_ref,
                   q_ref, k_ref, v_ref, qi_ref, u_ref, misc_ref, kb_ref, vb_ref):
    h = _rmsnorm_mod(x_ref[0], g_ref[...], scale_ref[0], shift_ref[0]).astype(BF16)

    def proj(c0, width):
        return jnp.dot(h, w_ref[:, c0:c0 + width], preferred_element_type=F32)

    q_ref[0] = (proj(_Q0, ATTN_WIDTH) * Q_PRESCALE).astype(BF16)
    k = proj(_K0, ATTN_WIDTH)
    k_ref[0] = k
    kb_ref[0] = k.astype(BF16)
    v = proj(_V0, ATTN_WIDTH)
    v_ref[0] = v
    vb_ref[0] = v.astype(BF16)
    qi = proj(_QI0, IDX_HEADS * IDX_DIM).astype(BF16)
    for hh in range(IDX_HEADS):
        qi_ref[0, hh] = qi[:, hh * IDX_DIM:(hh + 1) * IDX_DIM]
    u_ref[0] = proj(_U0, POOL_WIDTH)
    misc_ref[0] = proj(_MISC0, MISC_WIDTH)


def _inproj(x, shift1, scale1, g1, w_packed, tm):
    bsz, t, d = x.shape
    tok = lambda width, dt: (jax.ShapeDtypeStruct((bsz, t, width), dt),
                             pl.BlockSpec((1, tm, width), lambda b, i: (b, i, 0)))
    outs = [tok(ATTN_WIDTH, BF16), tok(ATTN_WIDTH, F32), tok(ATTN_WIDTH, F32),
            (jax.ShapeDtypeStruct((bsz, IDX_HEADS, t, IDX_DIM), BF16),
             pl.BlockSpec((1, IDX_HEADS, tm, IDX_DIM), lambda b, i: (b, 0, i, 0))),
            tok(POOL_WIDTH, F32), tok(MISC_WIDTH, F32), tok(ATTN_WIDTH, BF16), tok(ATTN_WIDTH, BF16)]
    return pl.pallas_call(
        _inproj_kernel,
        grid=(bsz, t // tm),
        in_specs=[pl.BlockSpec((1, tm, d), lambda b, i: (b, i, 0)),
                  _mod_spec(shift1, tm), _mod_spec(scale1, tm),
                  pl.BlockSpec((1, d), lambda b, i: (0, 0)),
                  _resident((d, _IN_PACKED), lambda b, i: (0, 0))],
        out_specs=[o[1] for o in outs],
        out_shape=[o[0] for o in outs],
        compiler_params=_params(("arbitrary", "arbitrary")),
        name="inproj",
    )(x, shift1, scale1, g1.reshape(1, d), w_packed)


def _key_to_float(u):
    bits = jnp.where(u < 0, u & jnp.int32(0x7FFFFFFF), ~u)
    return lax.bitcast_convert_type(bits, F32)


def _dsa_kernel(q_ref, qi_ref, wt_ref, ki_ref, k_ref, vt_ref, o_ref,
                sc_ref, m_ref, alpha_ref, acc_ref, s_ref, p_ref, *, tq, ts, tk, pos0, topk):
    i = pl.program_id(1)
    s_pad = sc_ref.shape[0]
    nt_dims = (((1,), (1,)), ((), ()))
    lim = jnp.minimum(((pos0 + (i + 1) * tq - 1) // CHUNK + 1) * CHUNK, s_pad)
    n_s = (lim + ts - 1) // ts
    n_k = (lim + tk - 1) // tk
    qpos = pos0 + i * tq + lax.broadcasted_iota(jnp.int32, (1, tq), 1)
    klim = (qpos // CHUNK + 1) * CHUNK
    wv = wt_ref[0] * (IDX_HEADS ** -0.5 * IDX_DIM ** -0.5)

    def key_index(s0, rows):
        return s0 + lax.broadcasted_iota(jnp.int32, (rows, tq), 0)

    def score_tile(t, carry):
        smax, smin = carry
        s0 = pl.multiple_of(t * ts, ts)
        kt = ki_ref[0, pl.ds(s0, ts), :]
        acc = jnp.zeros((ts, tq), F32)
        for hh in range(IDX_HEADS):
            d = lax.dot_general(kt, qi_ref[0, hh], nt_dims, preferred_element_type=F32)
            acc = acc + jnp.maximum(d, 0.0) * wv[hh:hh + 1, :]
        adm = key_index(s0, ts) < klim
        sc_ref[pl.ds(s0, ts), :] = jnp.where(adm, acc, NEG_INF)
        smax = jnp.maximum(smax, jnp.max(jnp.where(adm, acc, -F32_MAX), axis=0, keepdims=True))
        smin = jnp.minimum(smin, jnp.min(jnp.where(adm, acc, F32_MAX), axis=0, keepdims=True))
        return smax, smin

    smax, smin = lax.fori_loop(0, n_s, score_tile,
                               (jnp.full((1, tq), -F32_MAX, F32), jnp.full((1, tq), F32_MAX, F32)))

    def count(pred):
        def body(t, c):
            s0 = pl.multiple_of(t * ts, ts)
            ind = jnp.where(pred(sc_ref[pl.ds(s0, ts), :], s0), 1.0, 0.0)
            return c + jnp.sum(ind.reshape(ts // 8, 8, tq), axis=0)
        c = lax.fori_loop(0, n_s, body, jnp.zeros((8, tq), F32))
        return jnp.sum(c, axis=0, keepdims=True)

    def write_bias(sel_fn):
        def body(t, carry):
            s0 = pl.multiple_of(t * ts, ts)
            x = sc_ref[pl.ds(s0, ts), :]
            sc_ref[pl.ds(s0, ts), :] = jnp.where(sel_fn(x, s0), 0.0, NEG_INF)
            return carry
        lax.fori_loop(0, n_s, body, 0)

        def fill(t, carry):
            sc_ref[pl.ds(pl.multiple_of(t * ts, ts), ts), :] = jnp.full((ts, tq), NEG_INF, F32)
            return carry
        lax.fori_loop(n_s, n_k * (tk // ts), fill, 0)

    kf = float(topk)
    keep_all = klim <= topk

    def search_cond(st):
        it, _, _, _, _, _, done = st
        return jnp.logical_and(it < SEARCH_MAX_ITERS, jnp.min(done) < 0.5)

    def search_body(st):
        it, lo, hi, clo, chi, tau, done = st
        frac = (jnp.log(clo) - np.log(kf)) / (jnp.log(clo) - jnp.log(jnp.maximum(chi, 0.5)))
        frac = jnp.where(it % 3 == 2, 0.5, jnp.clip(frac, 0.02, 0.98))
        x = lo + frac * (hi - lo)
        c = count(lambda s, _: s >= x)
        hit = c == kf
        tau = jnp.where(jnp.logical_and(hit, done < 0.5), x, tau)
        done = jnp.where(hit, 1.0, done)
        above = c > kf
        lo = jnp.where(above, x, lo)
        clo = jnp.where(above, c, clo)
        hi = jnp.where(above, hi, x)
        chi = jnp.where(above, chi, c)
        return it + 1, lo, hi, clo, chi, tau, done

    n_adm = jnp.minimum(klim, s_pad).astype(F32)
    st0 = (jnp.int32(0), smin, smax, n_adm, jnp.ones((1, tq), F32),
           jnp.full((1, tq), VALID_MIN, F32), jnp.where(keep_all, 1.0, 0.0))
    _, _, _, _, _, tau_fast, done = lax.while_loop(search_cond, search_body, st0)
    converged = jnp.min(done) > 0.5

    @pl.when(converged)
    def _():
        tau_eff = jnp.maximum(tau_fast, VALID_MIN)
        write_bias(lambda x, _: x >= tau_eff)

    @pl.when(jnp.logical_not(converged))
    def _():
        def bit_body(b, carry):
            u, cnt_u = carry
            trial = u | lax.shift_left(jnp.int32(1), 31 - b)
            f = _key_to_float(trial)
            c = count(lambda s, _: s >= f)
            ok = c >= kf
            return jnp.where(ok, trial, u), jnp.where(ok, c, cnt_u)

        u, cnt_ge = lax.fori_loop(0, 32, bit_body,
                                  (jnp.zeros((1, tq), jnp.int32), jnp.zeros((1, tq), F32)))
        tau = _key_to_float(u)
        tie = jnp.logical_and(cnt_ge > kf, tau >= VALID_MIN)
        need = kf - count(lambda s, _: s > tau)
        nbits = s_pad.bit_length()

        def idx_body(b, lo):
            trial = lo | lax.shift_left(jnp.int32(1), (nbits - 1) - b)
            c = count(lambda s, s0: jnp.logical_and(s == tau, key_index(s0, ts) < trial))
            return jnp.where(c < need, trial, lo)

        lo = lax.fori_loop(0, nbits, idx_body, jnp.zeros((1, tq), jnp.int32))
        jstar = jnp.where(tie, lo + 1, s_pad)

        def sel(s, s0):
            keep = jnp.logical_or(s > tau, jnp.logical_and(s == tau, key_index(s0, ts) < jstar))
            return jnp.logical_and(keep, s >= VALID_MIN)
        write_bias(sel)

    m_ref[...] = jnp.full(m_ref.shape, NEG_INF, F32)
    acc_ref[...] = jnp.zeros(acc_ref.shape, F32)

    def kv_body(t, carry):
        k0 = pl.multiple_of(t * tk, tk)
        bias = sc_ref[pl.ds(k0, tk), :]
        for hh in range(N_HEADS):
            hs = slice(hh * HEAD_DIM, (hh + 1) * HEAD_DIM)
            s = lax.dot_general(k_ref[0, pl.ds(k0, tk), hs], q_ref[0, :, hs], nt_dims,
                                preferred_element_type=F32) + bias
            s_ref[hh] = s
            m_prev = m_ref[hh]
            m_new = jnp.maximum(m_prev, jnp.max(s, axis=0, keepdims=True))
            alpha_ref[hh] = jnp.exp2(m_prev - m_new)
            m_ref[hh] = m_new
        for hh in range(N_HEADS):
            p_ref[hh] = jnp.exp2(s_ref[hh] - m_ref[hh]).astype(BF16)
        for hh in range(N_HEADS):
            vs = slice(hh * V_ROWS, (hh + 1) * V_ROWS)
            acc_ref[vs, :] = alpha_ref[hh] * acc_ref[vs, :] + jnp.dot(
                vt_ref[0, vs, pl.ds(k0, tk)], p_ref[hh], preferred_element_type=F32)
        return carry

    lax.fori_loop(0, n_k, kv_body, 0)
    for hh in range(N_HEADS):
        r0 = hh * V_ROWS
        inv_l = 1.0 / acc_ref[r0 + HEAD_DIM:r0 + HEAD_DIM + 1, :]
        o_ref[0, :, hh * HEAD_DIM:(hh + 1) * HEAD_DIM] = (
            acc_ref[r0:r0 + HEAD_DIM, :] * inv_l).T.astype(o_ref.dtype)


def _values_t(vb):
    bsz, s, _ = vb.shape
    vt = jnp.transpose(vb.reshape(bsz, s, N_HEADS, HEAD_DIM), (0, 2, 3, 1))
    ones = jnp.ones((bsz, N_HEADS, V_ROWS - HEAD_DIM, s), vb.dtype)
    return jnp.concatenate([vt, ones], axis=2).reshape(bsz, N_HEADS * V_ROWS, s)


def _dsa(q, qi, wt, kib, kb, vt, *, pos0, topk, tq, ts, tk):
    bsz, t, _ = q.shape
    s_pad = kb.shape[1]
    assert s_pad % tk == 0 and tk % ts == 0 and ts >= topk and t % tq == 0
    kern = functools.partial(_dsa_kernel, tq=tq, ts=ts, tk=tk, pos0=pos0, topk=topk)
    return pl.pallas_call(
        kern,
        grid=(bsz, t // tq),
        in_specs=[pl.BlockSpec((1, tq, ATTN_WIDTH), lambda b, i: (b, i, 0)),
                  pl.BlockSpec((1, IDX_HEADS, tq, IDX_DIM), lambda b, i: (b, 0, i, 0)),
                  pl.BlockSpec((1, IDX_HEADS, tq), lambda b, i: (b, 0, i)),
                  _resident((1, s_pad, IDX_DIM), lambda b, i: (b, 0, 0)),
                  _resident((1, s_pad, ATTN_WIDTH), lambda b, i: (b, 0, 0)),
                  _resident((1, N_HEADS * V_ROWS, s_pad), lambda b, i: (b, 0, 0))],
        out_specs=pl.BlockSpec((1, tq, ATTN_WIDTH), lambda b, i: (b, i, 0)),
        out_shape=jax.ShapeDtypeStruct((bsz, t, ATTN_WIDTH), BF16),
        scratch_shapes=[pltpu.VMEM((s_pad, tq), F32),
                        pltpu.VMEM((N_HEADS, 1, tq), F32),
                        pltpu.VMEM((N_HEADS, 1, tq), F32),
                        pltpu.VMEM((N_HEADS * V_ROWS, tq), F32),
                        pltpu.VMEM((N_HEADS, tk, tq), F32),
                        pltpu.VMEM((N_HEADS, tk, tq), BF16)],
        compiler_params=_params(("arbitrary", "arbitrary")),
        name="dsa",
    )(q, qi, wt, kib, kb, vt)


def _mix_kernel(x_ref, attn_ref, u_ref, halo_ref, gate1_ref, shift2_ref, scale2_ref, g2_ref,
                wpool_ref, pscale_ref, wout_ref, x1_ref, h2_ref, xp_ref, *, tm, pos0, zero_first_halo):
    i = pl.program_id(1)
    halo = halo_ref[0]
    if zero_first_halo:
        halo = jnp.where(i == 0, 0.0, halo)
    xp_ref[0:HALO] = halo
    xp_ref[HALO:HALO + tm] = u_ref[0]
    pos = pos0 + i * tm + lax.broadcasted_iota(jnp.int32, (tm, 1), 0)
    mix = jnp.dot(attn_ref[0], wout_ref[0:ATTN_WIDTH], preferred_element_type=F32)
    for g, w in enumerate(POOL_WINDOWS):
        sl = slice(g * POOL_GROUP_DIM, (g + 1) * POOL_GROUP_DIM)
        cur = xp_ref[HALO:HALO + tm, sl]
        wsum = cur
        for dlt in range(1, w):
            wsum = wsum + xp_ref[HALO - dlt:HALO - dlt + tm, sl]
        cnt = jnp.minimum(pos + 1, w).astype(F32)
        z = wsum / cnt - cur
        y = jnp.dot(z.astype(BF16), wpool_ref[g], preferred_element_type=F32) * pscale_ref[:, sl]
        mix = mix + jnp.dot(y.astype(BF16), wout_ref[ATTN_WIDTH + g * POOL_GROUP_DIM:
                                                      ATTN_WIDTH + (g + 1) * POOL_GROUP_DIM],
                            preferred_element_type=F32)
    x1 = x_ref[0] + gate1_ref[0] * mix
    x1_ref[0] = x1
    h2_ref[0] = _rmsnorm_mod(x1, g2_ref[...], scale2_ref[0], shift2_ref[0]).astype(BF16)


def _mix(x, attn, u, halo, gate1, shift2, scale2, g2, wpool, pscale, wout, *, tm, pos0, halo_from_u):
    bsz, t, d = x.shape
    if halo_from_u:
        halo_spec = pl.BlockSpec((1, HALO, POOL_WIDTH),
                                 lambda b, i: (b, jnp.maximum(i * (tm // HALO) - 1, 0), 0))
    else:
        halo_spec = pl.BlockSpec((1, HALO, POOL_WIDTH), lambda b, i: (b, 0, 0))
    kern = functools.partial(_mix_kernel, tm=tm, pos0=pos0, zero_first_halo=halo_from_u)
    tokspec = lambda width: pl.BlockSpec((1, tm, width), lambda b, i: (b, i, 0))
    return pl.pallas_call(
        kern,
        grid=(bsz, t // tm),
        in_specs=[tokspec(d), tokspec(ATTN_WIDTH), tokspec(POOL_WIDTH), halo_spec,
                  _mod_spec(gate1, tm), _mod_spec(shift2, tm), _mod_spec(scale2, tm),
                  pl.BlockSpec((1, d), lambda b, i: (0, 0)),
                  _resident(wpool.shape, lambda b, i: (0, 0, 0)),
                  pl.BlockSpec((1, POOL_WIDTH), lambda b, i: (0, 0)),
                  _resident(wout.shape, lambda b, i: (0, 0))],
        out_specs=[tokspec(d), tokspec(d)],
        out_shape=[jax.ShapeDtypeStruct((bsz, t, d), F32), jax.ShapeDtypeStruct((bsz, t, d), BF16)],
        scratch_shapes=[pltpu.VMEM((HALO + tm, POOL_WIDTH), F32)],
        compiler_params=_params(("arbitrary", "arbitrary")),
        name="mix",
    )(x, attn, u, halo, gate1, shift2, scale2, g2.reshape(1, d), wpool, pscale.reshape(1, POOL_WIDTH), wout)


def _ffn_kernel(h_ref, x1_ref, gate2_ref, gf_ref, wg_ref, wu_ref, wd_ref, y_ref, acc_ref):
    j = pl.program_id(2)

    @pl.when(j == 0)
    def _():
        acc_ref[...] = jnp.zeros(acc_ref.shape, F32)

    h = h_ref[0]
    g = jnp.dot(h, wg_ref[...], preferred_element_type=F32)
    up = jnp.dot(h, wu_ref[...], preferred_element_type=F32)
    a = (g * jax.nn.sigmoid(g) * up).astype(BF16)
    acc_ref[...] += jnp.dot(a, wd_ref[...], preferred_element_type=F32)

    @pl.when(j == pl.num_programs(2) - 1)
    def _():
        out = x1_ref[0] + gate2_ref[0] * acc_ref[...]
        y_ref[0] = out * lax.rsqrt(jnp.mean(out * out, axis=-1, keepdims=True) + EPS) * gf_ref[...]


def _ffn(h2, x1, gate2, g_final, wg, wu, wd, *, tm, tf):
    bsz, t, d = x1.shape
    dff = wg.shape[1]
    tokspec = pl.BlockSpec((1, tm, d), lambda b, i, j: (b, i, 0))
    return pl.pallas_call(
        _ffn_kernel,
        grid=(bsz, t // tm, dff // tf),
        in_specs=[tokspec, tokspec, _mod_spec(gate2, tm),
                  pl.BlockSpec((1, d), lambda b, i, j: (0, 0)),
                  pl.BlockSpec((d, tf), lambda b, i, j: (0, j)),
                  pl.BlockSpec((d, tf), lambda b, i, j: (0, j)),
                  pl.BlockSpec((tf, d), lambda b, i, j: (j, 0))],
        out_specs=tokspec,
        out_shape=jax.ShapeDtypeStruct((bsz, t, d), F32),
        scratch_shapes=[pltpu.VMEM((tm, d), F32)],
        compiler_params=_params(("arbitrary", "arbitrary", "arbitrary")),
        name="ffn",
    )(h2, x1, gate2, g_final.reshape(1, d), wg, wu, wd)


DSA_KEY_TILE = 128
DSA_ATTN_TILE = 512


def _pad_keys(a, s_pad, axis):
    pad = [(0, 0)] * a.ndim
    pad[axis] = (0, s_pad - a.shape[axis])
    return jnp.pad(a, pad)


def kernel(x_prompt, x_sample, c_prompt, c_sample, cache_k, cache_v, cache_kidx, state_pool,
           w_ada, b_ada, g_norm1, w_in, w_pool, pool_scale, w_out, g_norm2, w_gate, w_up, w_down, g_final):
    bp, tp, d = x_prompt.shape
    bs, tsmp, _ = x_sample.shape
    depth = w_ada.shape[0]
    past = cache_k.shape[2]
    assert depth == 1 and d == 2 * ATTN_WIDTH
    topk_p = min(TOPK_MAX, tp // 4)
    topk_s = min(TOPK_MAX, (past + tsmp) // 4)
    l = 0

    w = w_in[l]
    w_packed = jnp.concatenate(
        [w[:, :4096], w[:, 4176:5200], w[:, 4096:4176], jnp.zeros((d, _IN_PACKED - 5200), w.dtype)],
        axis=1).astype(BF16)
    wpool_b = w_pool[l].astype(BF16)
    wout_b = w_out[l].astype(BF16)
    wg_b, wu_b, wd_b = w_gate[l].astype(BF16), w_up[l].astype(BF16), w_down[l].astype(BF16)

    rows = bp + bs
    c_all = jnp.concatenate([c_prompt, c_sample], axis=0)
    c_all = jnp.pad(c_all, ((0, -rows % 8), (0, 0)))
    mod = _ada(c_all, w_ada[l], b_ada[l])
    mods_p = [m[:bp, None, :] for m in jnp.split(mod, 6, axis=-1)]
    mods_s = [jnp.repeat(m[bp:rows], tsmp, axis=0)[None] for m in jnp.split(mod, 6, axis=-1)]
    mods_sb = [m[bp:rows, None, :] for m in jnp.split(mod, 6, axis=-1)]
    head_weights_t = lambda misc: jnp.swapaxes(misc[..., IDX_DIM:IDX_DIM + IDX_HEADS], 1, 2)

    q, k, v, qi, u, misc, kb, vb = _inproj(x_prompt, mods_p[0], mods_p[1], g_norm1[l], w_packed, tm=256)
    attn = _dsa(q, qi, head_weights_t(misc), misc[..., :IDX_DIM].astype(BF16), kb, _values_t(vb),
                pos0=0, topk=topk_p, tq=min(256, tp), ts=max(DSA_KEY_TILE, topk_p), tk=min(DSA_ATTN_TILE, tp))
    x1, h2 = _mix(x_prompt, attn, u, u, mods_p[2], mods_p[3], mods_p[4], g_norm2[l],
                  wpool_b, pool_scale[l], wout_b, tm=512, pos0=0, halo_from_u=True)
    y_prompt = _ffn(h2, x1, mods_p[5], g_final, wg_b, wu_b, wd_b, tm=512, tf=512)

    ntok = bs * tsmp
    xs_flat = x_sample.reshape(1, ntok, d)
    q_s, k_s, v_s, qi_s, u_s, misc_s, kb_s, vb_s = _inproj(
        xs_flat, mods_s[0], mods_s[1], g_norm1[l], w_packed, tm=min(256, ntok))
    per_batch = lambda a: a.reshape(bs, tsmp, a.shape[-1])
    q_s, k_s, v_s, u_s, misc_s, kb_s, vb_s = map(per_batch, (q_s, k_s, v_s, u_s, misc_s, kb_s, vb_s))
    qi_s = jnp.swapaxes(qi_s.reshape(IDX_HEADS, bs, tsmp, IDX_DIM), 0, 1)
    s_all = past + tsmp
    ts_s = max(DSA_KEY_TILE, topk_s)
    tk_s = min(DSA_ATTN_TILE, -(-s_all // ts_s) * ts_s)
    pad_step = int(np.lcm(ts_s, tk_s))
    s_pad = -(-s_all // pad_step) * pad_step
    k_all = _pad_keys(jnp.concatenate([cache_k[l].reshape(bs, past, ATTN_WIDTH).astype(BF16), kb_s], axis=1), s_pad, 1)
    v_all = _pad_keys(jnp.concatenate([cache_v[l].reshape(bs, past, ATTN_WIDTH).astype(BF16), vb_s], axis=1), s_pad, 1)
    ki_all = _pad_keys(jnp.concatenate([cache_kidx[l], misc_s[..., :IDX_DIM]], axis=1).astype(BF16), s_pad, 1)
    attn_s = _dsa(q_s, qi_s, head_weights_t(misc_s), ki_all, k_all, _values_t(v_all),
                  pos0=past, topk=topk_s, tq=tsmp, ts=ts_s, tk=tk_s)
    halo_s = jnp.pad(state_pool[l], ((0, 0), (HALO - POOL_PAD, 0), (0, 0)))
    x1_s, h2_s = _mix(x_sample, attn_s, u_s, halo_s, mods_sb[2], mods_sb[3], mods_sb[4], g_norm2[l],
                      wpool_b, pool_scale[l], wout_b, tm=tsmp, pos0=past, halo_from_u=False)
    y_sample = _ffn(h2_s.reshape(1, ntok, d), x1_s.reshape(1, ntok, d), mods_s[5], g_final,
                    wg_b, wu_b, wd_b, tm=min(256, ntok), tf=512).reshape(bs, tsmp, d)

    heads = lambda a: a.reshape(1, a.shape[0], a.shape[1], N_HEADS, HEAD_DIM)
    assert tp >= POOL_PAD and tsmp >= POOL_PAD
    pool_p = u[:, -POOL_PAD:]
    pool_s = u_s[:, -POOL_PAD:]
    return (y_prompt, y_sample,
            heads(k), heads(v), misc[..., :IDX_DIM][None], pool_p[None],
            heads(k_s), heads(v_s), misc_s[..., :IDX_DIM][None], pool_s[None])
```

```python
import functools

import numpy as np
import jax
import jax.numpy as jnp
from jax import lax
from jax.experimental import pallas as pl
from jax.experimental.pallas import tpu as pltpu

F32 = jnp.float32
BF16 = jnp.bfloat16

CHUNK = 64
N_HEADS = 8
HEAD_DIM = 128
ATTN_WIDTH = N_HEADS * HEAD_DIM
IDX_HEADS = 16
IDX_DIM = 64
TOPK_MAX = 256
POOL_WINDOWS = (2, 4, 8, 16)
POOL_GROUP_DIM = 256
POOL_WIDTH = POOL_GROUP_DIM * len(POOL_WINDOWS)
POOL_PAD = max(POOL_WINDOWS) - 1
HALO = 16
NEG_INF = -1e30
EPS = 1e-6
VALID_MIN = float(np.nextafter(np.float32(0.5) * np.float32(NEG_INF), np.float32(0.0)))
F32_MAX = float(np.finfo(np.float32).max)
MISC_WIDTH = 128
Q_PRESCALE = float(HEAD_DIM ** -0.5 * np.log2(np.e))
V_ROWS = HEAD_DIM + 16
SEARCH_MAX_ITERS = 24
SCORE_SUB_ROWS = 128
V7X_VMEM_BYTES = 64 * 1024 * 1024
VMEM_LIMIT = V7X_VMEM_BYTES - 8 * 1024 * 1024


def _params(sem):
    return pltpu.CompilerParams(dimension_semantics=sem, vmem_limit_bytes=VMEM_LIMIT)


def _resident(block_shape, index_map):
    return pl.BlockSpec(block_shape, index_map, pipeline_mode=pl.Buffered(1))


def _mod_spec(mod, tm):
    d = mod.shape[-1]
    if mod.shape[1] == 1:
        return pl.BlockSpec((1, 1, d), lambda b, i, *_: (b, 0, 0))
    return pl.BlockSpec((1, tm, d), lambda b, i, *_: (b, i, 0))


def _ada_kernel(c_ref, w_ref, b_ref, o_ref):
    c = c_ref[...]
    a = (c * jax.nn.sigmoid(c)).astype(BF16)
    o_ref[...] = jnp.dot(a, w_ref[...].astype(BF16), preferred_element_type=F32) + b_ref[...]


def _ada(c, w_ada, b_ada):
    rows, d = c.shape
    n = w_ada.shape[1]
    tn = 1536
    return pl.pallas_call(
        _ada_kernel,
        grid=(n // tn,),
        in_specs=[pl.BlockSpec((rows, d), lambda j: (0, 0)),
                  pl.BlockSpec((d, tn), lambda j: (0, j)),
                  pl.BlockSpec((1, tn), lambda j: (0, j))],
        out_specs=pl.BlockSpec((rows, tn), lambda j: (0, j)),
        out_shape=jax.ShapeDtypeStruct((rows, n), F32),
        compiler_params=_params(("arbitrary",)),
        name="ada",
    )(c, w_ada, b_ada.reshape(1, n))


_Q0, _K0, _V0, _QI0, _U0, _MISC0, _IN_PACKED = 0, 1024, 2048, 3072, 4096, 5120, 5248


def _rmsnorm_mod(x, g, scale, shift):
    y = x * lax.rsqrt(jnp.mean(x * x, axis=-1, keepdims=True) + EPS) * g
    return y * (1.0 + scale) + shift


def _inproj_kernel(x_ref, shift_ref, scale_ref, g_ref, w_ref,
                   q_ref, k_ref, v_ref, qi_ref, u_ref, misc_ref, kb_ref, vb_ref):
    h = _rmsnorm_mod(x_ref[0], g_ref[...], scale_ref[0], shift_ref[0]).astype(BF16)

    def proj(c0, width):
        return jnp.dot(h, w_ref[:, c0:c0 + width], preferred_element_type=F32)

    q_ref[0] = (proj(_Q0, ATTN_WIDTH) * Q_PRESCALE).astype(BF16)
    k = proj(_K0, ATTN_WIDTH)
    k_ref[0] = k
    kb_ref[0] = k.astype(BF16)
    v = proj(_V0, ATTN_WIDTH)
    v_ref[0] = v
    vb_ref[0] = v.astype(BF16)
    qi = proj(_QI0, IDX_HEADS * IDX_DIM).astype(BF16)
    for hh in range(IDX_HEADS):
        qi_ref[0, hh] = qi[:, hh * IDX_DIM:(hh + 1) * IDX_DIM]
    u_ref[0] = proj(_U0, POOL_WIDTH)
    misc_ref[0] = proj(_MISC0, MISC_WIDTH)


def _inproj(x, shift1, scale1, g1, w_packed, tm):
    bsz, t, d = x.shape
    tok = lambda width, dt: (jax.ShapeDtypeStruct((bsz, t, width), dt),
                             pl.BlockSpec((1, tm, width), lambda b, i: (b, i, 0)))
    outs = [tok(ATTN_WIDTH, BF16), tok(ATTN_WIDTH, F32), tok(ATTN_WIDTH, F32),
            (jax.ShapeDtypeStruct((bsz, IDX_HEADS, t, IDX_DIM), BF16),
             pl.BlockSpec((1, IDX_HEADS, tm, IDX_DIM), lambda b, i: (b, 0, i, 0))),
            tok(POOL_WIDTH, F32), tok(MISC_WIDTH, F32), tok(ATTN_WIDTH, BF16), tok(ATTN_WIDTH, BF16)]
    return pl.pallas_call(
        _inproj_kernel,
        grid=(bsz, t // tm),
        in_specs=[pl.BlockSpec((1, tm, d), lambda b, i: (b, i, 0)),
                  _mod_spec(shift1, tm), _mod_spec(scale1, tm),
                  pl.BlockSpec((1, d), lambda b, i: (0, 0)),
                  _resident((d, _IN_PACKED), lambda b, i: (0, 0))],
        out_specs=[o[1] for o in outs],
        out_shape=[o[0] for o in outs],
        compiler_params=_params(("arbitrary", "arbitrary")),
        name="inproj",
    )(x, shift1, scale1, g1.reshape(1, d), w_packed)


def _key_to_float(u):
    bits = jnp.where(u < 0, u & jnp.int32(0x7FFFFFFF), ~u)
    return lax.bitcast_convert_type(bits, F32)


def _dsa_kernel(q_ref, qi_ref, wt_ref, ki_ref, k_ref, vt_ref, o_ref,
                sc_ref, m_ref, acc_ref, s_ref, stat_ref, *, tq, ts, tk, pos0, topk):
    i = pl.program_id(1)
    s_pad = sc_ref.shape[0]
    sub_rows = min(ts, SCORE_SUB_ROWS)
    nt_dims = (((1,), (1,)), ((), ()))
    lim = jnp.minimum(((pos0 + (i + 1) * tq - 1) // CHUNK + 1) * CHUNK, s_pad)
    n_s = (lim + ts - 1) // ts
    qpos = pos0 + i * tq + lax.broadcasted_iota(jnp.int32, (1, tq), 1)
    klim = (qpos // CHUNK + 1) * CHUNK
    wv = wt_ref[0] * (IDX_HEADS ** -0.5 * IDX_DIM ** -0.5)

    def key_index(s0, rows):
        return s0 + lax.broadcasted_iota(jnp.int32, (rows, tq), 0)

    def score_tile(t, carry):
        smax, smin = carry
        for sub in range(ts // sub_rows):
            s0 = pl.multiple_of(t * ts + sub * sub_rows, sub_rows)
            kt = ki_ref[0, pl.ds(s0, sub_rows), :]
            acc = jnp.zeros((sub_rows, tq), F32)
            for hh in range(IDX_HEADS):
                d = lax.dot_general(kt, qi_ref[0, hh], nt_dims, preferred_element_type=F32)
                acc = acc + jnp.maximum(d, 0.0) * wv[hh:hh + 1, :]
            adm = key_index(s0, sub_rows) < klim
            sc_ref[pl.ds(s0, sub_rows), :] = jnp.where(adm, acc, NEG_INF)
            smax = jnp.maximum(smax, jnp.max(jnp.where(adm, acc, -F32_MAX), axis=0, keepdims=True))
            smin = jnp.minimum(smin, jnp.min(jnp.where(adm, acc, F32_MAX), axis=0, keepdims=True))
        return smax, smin

    smax, smin = lax.fori_loop(0, n_s, score_tile,
                               (jnp.full((1, tq), -F32_MAX, F32), jnp.full((1, tq), F32_MAX, F32)))

    def count(pred):
        def body(t, c):
            parts = []
            for sub in range(ts // sub_rows):
                s0 = pl.multiple_of(t * ts + sub * sub_rows, sub_rows)
                ind = jnp.where(pred(sc_ref[pl.ds(s0, sub_rows), :], s0), 1.0, 0.0)
                parts.append(jnp.sum(ind.reshape(sub_rows // 8, 8, tq), axis=0))
            return c + sum(parts)
        c = lax.fori_loop(0, n_s, body, jnp.zeros((8, tq), F32))
        return jnp.sum(c, axis=0, keepdims=True)

    def write_bias(sel_fn):
        def body(t, carry):
            for sub in range(ts // sub_rows):
                s0 = pl.multiple_of(t * ts + sub * sub_rows, sub_rows)
                x = sc_ref[pl.ds(s0, sub_rows), :]
                sc_ref[pl.ds(s0, sub_rows), :] = jnp.where(sel_fn(x, s0), 0.0, NEG_INF)
            return carry
        lax.fori_loop(0, n_s, body, 0)

        @pl.when(n_s * ts < s_pad)
        def _():
            sc_ref[pl.ds(pl.multiple_of(n_s * ts, ts), tk), :] = jnp.full((tk, tq), NEG_INF, F32)

    kf = float(topk)
    keep_all = klim <= topk

    def search_cond(st):
        it, _, _, _, _, _, done = st
        return jnp.logical_and(it < SEARCH_MAX_ITERS, jnp.min(done) < 0.5)

    def search_body(st):
        it, lo, hi, clo, chi, tau, done = st
        frac = (jnp.log(clo) - np.log(kf)) / (jnp.log(clo) - jnp.log(jnp.maximum(chi, 0.5)))
        frac = jnp.where(it % 3 == 2, 0.5, jnp.clip(frac, 0.02, 0.98))
        x = lo + frac * (hi - lo)
        c = count(lambda s, _: s >= x)
        hit = c == kf
        tau = jnp.where(jnp.logical_and(hit, done < 0.5), x, tau)
        done = jnp.where(hit, 1.0, done)
        above = c > kf
        lo = jnp.where(above, x, lo)
        clo = jnp.where(above, c, clo)
        hi = jnp.where(above, hi, x)
        chi = jnp.where(above, chi, c)
        return it + 1, lo, hi, clo, chi, tau, done

    n_adm = jnp.minimum(klim, s_pad).astype(F32)
    st0 = (jnp.int32(0), smin, smax, n_adm, jnp.ones((1, tq), F32),
           jnp.full((1, tq), VALID_MIN, F32), jnp.where(keep_all, 1.0, 0.0))
    _, _, _, _, _, tau_fast, done = lax.while_loop(search_cond, search_body, st0)
    converged = jnp.min(done) > 0.5

    @pl.when(converged)
    def _():
        tau_eff = jnp.maximum(tau_fast, VALID_MIN)
        write_bias(lambda x, _: x >= tau_eff)

    @pl.when(jnp.logical_not(converged))
    def _():
        def bit_body(b, carry):
            u, cnt_u = carry
            trial = u | lax.shift_left(jnp.int32(1), 31 - b)
            f = _key_to_float(trial)
            c = count(lambda s, _: s >= f)
            ok = c >= kf
            return jnp.where(ok, trial, u), jnp.where(ok, c, cnt_u)

        u, cnt_ge = lax.fori_loop(0, 32, bit_body,
                                  (jnp.zeros((1, tq), jnp.int32), jnp.zeros((1, tq), F32)))
        tau = _key_to_float(u)
        tie = jnp.logical_and(cnt_ge > kf, tau >= VALID_MIN)
        need = kf - count(lambda s, _: s > tau)
        nbits = s_pad.bit_length()

        def idx_body(b, lo):
            trial = lo | lax.shift_left(jnp.int32(1), (nbits - 1) - b)
            c = count(lambda s, s0: jnp.logical_and(s == tau, key_index(s0, s.shape[0]) < trial))
            return jnp.where(c < need, trial, lo)

        lo = lax.fori_loop(0, nbits, idx_body, jnp.zeros((1, tq), jnp.int32))
        jstar = jnp.where(tie, lo + 1, s_pad)

        def sel(s, s0):
            keep = jnp.logical_or(s > tau, jnp.logical_and(s == tau, key_index(s0, s.shape[0]) < jstar))
            return jnp.logical_and(keep, s >= VALID_MIN)
        write_bias(sel)

    m_ref[...] = jnp.full(m_ref.shape, NEG_INF, F32)
    acc_ref[...] = jnp.zeros(acc_ref.shape, F32)

    def logits(tile, hh, buf):
        k0 = pl.multiple_of(jnp.minimum(tile * tk, s_pad - tk), tk)
        hs = slice(hh * HEAD_DIM, (hh + 1) * HEAD_DIM)
        s = lax.dot_general(k_ref[0, pl.ds(k0, tk), hs], q_ref[0, :, hs], nt_dims,
                            preferred_element_type=F32) + sc_ref[pl.ds(k0, tk), :]
        s_ref[buf, hh] = s
        r = jnp.max(s.reshape(tk // 8, 8, tq), axis=0)
        for shift in (4, 2, 1):
            r = jnp.maximum(r, pltpu.roll(r, shift, 0))
        m_prev = m_ref[hh]
        m_new = jnp.maximum(m_prev, r)
        m_ref[hh] = m_new
        stat_ref[buf, 0, hh] = m_new
        stat_ref[buf, 1, hh] = jnp.exp2(m_prev - m_new)

    def accumulate(tile, hh, buf):
        k0 = pl.multiple_of(tile * tk, tk)
        vs = slice(hh * V_ROWS, (hh + 1) * V_ROWS)
        z = (s_ref[buf, hh].reshape(tk // 8, 8, tq) - stat_ref[buf, 0, hh]).reshape(tk, tq)
        pv = jnp.dot(vt_ref[0, vs, pl.ds(k0, tk)], jnp.exp2(z).astype(BF16), preferred_element_type=F32)
        acc = stat_ref[buf, 1, hh] * acc_ref[vs, :].reshape(V_ROWS // 8, 8, tq) + pv.reshape(V_ROWS // 8, 8, tq)
        acc_ref[vs, :] = acc.reshape(V_ROWS, tq)

    for hh in range(N_HEADS):
        logits(0, hh, 0)

    def pair_body(pr, carry):
        for hh in range(N_HEADS):
            logits(2 * pr + 1, hh, 1)
            accumulate(2 * pr, hh, 0)
        for hh in range(N_HEADS):
            logits(2 * pr + 2, hh, 0)
            accumulate(2 * pr + 1, hh, 1)
        return carry

    lax.fori_loop(0, n_s, pair_body, 0)
    for hh in range(N_HEADS):
        r0 = hh * V_ROWS
        inv_l = 1.0 / acc_ref[r0 + HEAD_DIM:r0 + HEAD_DIM + 1, :]
        o_ref[0, :, hh * HEAD_DIM:(hh + 1) * HEAD_DIM] = (
            acc_ref[r0:r0 + HEAD_DIM, :] * inv_l).T.astype(o_ref.dtype)


def _values_t(vb):
    bsz, s, _ = vb.shape
    vt = jnp.transpose(vb.reshape(bsz, s, N_HEADS, HEAD_DIM), (0, 2, 3, 1))
    ones = jnp.ones((bsz, N_HEADS, V_ROWS - HEAD_DIM, s), vb.dtype)
    return jnp.concatenate([vt, ones], axis=2).reshape(bsz, N_HEADS * V_ROWS, s)


def _dsa(q, qi, wt, kib, kb, vt, *, pos0, topk, tq, ts, tk):
    bsz, t, _ = q.shape
    s_pad = kb.shape[1]
    assert s_pad % ts == 0 and ts == 2 * tk and ts >= topk and t % tq == 0
    kern = functools.partial(_dsa_kernel, tq=tq, ts=ts, tk=tk, pos0=pos0, topk=topk)
    return pl.pallas_call(
        kern,
        grid=(bsz, t // tq),
        in_specs=[pl.BlockSpec((1, tq, ATTN_WIDTH), lambda b, i: (b, i, 0)),
                  pl.BlockSpec((1, IDX_HEADS, tq, IDX_DIM), lambda b, i: (b, 0, i, 0)),
                  pl.BlockSpec((1, IDX_HEADS, tq), lambda b, i: (b, 0, i)),
                  _resident((1, s_pad, IDX_DIM), lambda b, i: (b, 0, 0)),
                  _resident((1, s_pad, ATTN_WIDTH), lambda b, i: (b, 0, 0)),
                  _resident((1, N_HEADS * V_ROWS, s_pad), lambda b, i: (b, 0, 0))],
        out_specs=pl.BlockSpec((1, tq, ATTN_WIDTH), lambda b, i: (b, i, 0)),
        out_shape=jax.ShapeDtypeStruct((bsz, t, ATTN_WIDTH), BF16),
        scratch_shapes=[pltpu.VMEM((s_pad, tq), F32),
                        pltpu.VMEM((N_HEADS, 8, tq), F32),
                        pltpu.VMEM((N_HEADS * V_ROWS, tq), F32),
                        pltpu.VMEM((2, N_HEADS, tk, tq), F32),
                        pltpu.VMEM((2, 2, N_HEADS, 8, tq), F32)],
        compiler_params=_params(("arbitrary", "arbitrary")),
        name="dsa",
    )(q, qi, wt, kib, kb, vt)


def _mix_kernel(x_ref, attn_ref, u_ref, halo_ref, gate1_ref, shift2_ref, scale2_ref, g2_ref,
                wpool_ref, pscale_ref, wout_ref, x1_ref, h2_ref, xp_ref, *, tm, pos0, zero_first_halo):
    i = pl.program_id(1)
    halo = halo_ref[0]
    if zero_first_halo:
        halo = jnp.where(i == 0, 0.0, halo)
    xp_ref[0:HALO] = halo
    xp_ref[HALO:HALO + tm] = u_ref[0]
    pos = pos0 + i * tm + lax.broadcasted_iota(jnp.int32, (tm, 1), 0)
    mix = jnp.dot(attn_ref[0], wout_ref[0:ATTN_WIDTH], preferred_element_type=F32)
    for g, w in enumerate(POOL_WINDOWS):
        sl = slice(g * POOL_GROUP_DIM, (g + 1) * POOL_GROUP_DIM)
        cur = xp_ref[HALO:HALO + tm, sl]
        wsum = cur
        for dlt in range(1, w):
            wsum = wsum + xp_ref[HALO - dlt:HALO - dlt + tm, sl]
        cnt = jnp.minimum(pos + 1, w).astype(F32)
        z = wsum / cnt - cur
        y = jnp.dot(z.astype(BF16), wpool_ref[g], preferred_element_type=F32) * pscale_ref[:, sl]
        mix = mix + jnp.dot(y.astype(BF16), wout_ref[ATTN_WIDTH + g * POOL_GROUP_DIM:
                                                      ATTN_WIDTH + (g + 1) * POOL_GROUP_DIM],
                            preferred_element_type=F32)
    x1 = x_ref[0] + gate1_ref[0] * mix
    x1_ref[0] = x1
    h2_ref[0] = _rmsnorm_mod(x1, g2_ref[...], scale2_ref[0], shift2_ref[0]).astype(BF16)


def _mix(x, attn, u, halo, gate1, shift2, scale2, g2, wpool, pscale, wout, *, tm, pos0, halo_from_u):
    bsz, t, d = x.shape
    if halo_from_u:
        halo_spec = pl.BlockSpec((1, HALO, POOL_WIDTH),
                                 lambda b, i: (b, jnp.maximum(i * (tm // HALO) - 1, 0), 0))
    else:
        halo_spec = pl.BlockSpec((1, HALO, POOL_WIDTH), lambda b, i: (b, 0, 0))
    kern = functools.partial(_mix_kernel, tm=tm, pos0=pos0, zero_first_halo=halo_from_u)
    tokspec = lambda width: pl.BlockSpec((1, tm, width), lambda b, i: (b, i, 0))
    return pl.pallas_call(
        kern,
        grid=(bsz, t // tm),
        in_specs=[tokspec(d), tokspec(ATTN_WIDTH), tokspec(POOL_WIDTH), halo_spec,
                  _mod_spec(gate1, tm), _mod_spec(shift2, tm), _mod_spec(scale2, tm),
                  pl.BlockSpec((1, d), lambda b, i: (0, 0)),
                  _resident(wpool.shape, lambda b, i: (0, 0, 0)),
                  pl.BlockSpec((1, POOL_WIDTH), lambda b, i: (0, 0)),
                  _resident(wout.shape, lambda b, i: (0, 0))],
        out_specs=[tokspec(d), tokspec(d)],
        out_shape=[jax.ShapeDtypeStruct((bsz, t, d), F32), jax.ShapeDtypeStruct((bsz, t, d), BF16)],
        scratch_shapes=[pltpu.VMEM((HALO + tm, POOL_WIDTH), F32)],
        compiler_params=_params(("arbitrary", "arbitrary")),
        name="mix",
    )(x, attn, u, halo, gate1, shift2, scale2, g2.reshape(1, d), wpool, pscale.reshape(1, POOL_WIDTH), wout)


def _ffn_kernel(h_ref, x1_ref, gate2_ref, gf_ref, wg_ref, wu_ref, wd_ref, y_ref, acc_ref):
    j = pl.program_id(2)

    @pl.when(j == 0)
    def _():
        acc_ref[...] = jnp.zeros(acc_ref.shape, F32)

    h = h_ref[0]
    g = jnp.dot(h, wg_ref[...], preferred_element_type=F32)
    up = jnp.dot(h, wu_ref[...], preferred_element_type=F32)
    a = (g * jax.nn.sigmoid(g) * up).astype(BF16)
    acc_ref[...] += jnp.dot(a, wd_ref[...], preferred_element_type=F32)

    @pl.when(j == pl.num_programs(2) - 1)
    def _():
        out = x1_ref[0] + gate2_ref[0] * acc_ref[...]
        y_ref[0] = out * lax.rsqrt(jnp.mean(out * out, axis=-1, keepdims=True) + EPS) * gf_ref[...]


def _ffn(h2, x1, gate2, g_final, wg, wu, wd, *, tm, tf):
    bsz, t, d = x1.shape
    dff = wg.shape[1]
    tokspec = pl.BlockSpec((1, tm, d), lambda b, i, j: (b, i, 0))
    return pl.pallas_call(
        _ffn_kernel,
        grid=(bsz, t // tm, dff // tf),
        in_specs=[tokspec, tokspec, _mod_spec(gate2, tm),
                  pl.BlockSpec((1, d), lambda b, i, j: (0, 0)),
                  pl.BlockSpec((d, tf), lambda b, i, j: (0, j)),
                  pl.BlockSpec((d, tf), lambda b, i, j: (0, j)),
                  pl.BlockSpec((tf, d), lambda b, i, j: (j, 0))],
        out_specs=tokspec,
        out_shape=jax.ShapeDtypeStruct((bsz, t, d), F32),
        scratch_shapes=[pltpu.VMEM((tm, d), F32)],
        compiler_params=_params(("arbitrary", "arbitrary", "arbitrary")),
        name="ffn",
    )(h2, x1, gate2, g_final.reshape(1, d), wg, wu, wd)


DSA_KEY_TILE = 512


def _dsa_tiles(n_keys, topk):
    ts = max(min(DSA_KEY_TILE, n_keys), topk)
    ts = -(-ts // 256) * 256
    return ts, -(-n_keys // ts) * ts


def _pad_keys(a, s_pad, axis):
    pad = [(0, 0)] * a.ndim
    pad[axis] = (0, s_pad - a.shape[axis])
    return jnp.pad(a, pad)


def kernel(x_prompt, x_sample, c_prompt, c_sample, cache_k, cache_v, cache_kidx, state_pool,
           w_ada, b_ada, g_norm1, w_in, w_pool, pool_scale, w_out, g_norm2, w_gate, w_up, w_down, g_final):
    bp, tp, d = x_prompt.shape
    bs, tsmp, _ = x_sample.shape
    depth = w_ada.shape[0]
    past = cache_k.shape[2]
    assert depth == 1 and d == 2 * ATTN_WIDTH
    topk_p = min(TOPK_MAX, tp // 4)
    topk_s = min(TOPK_MAX, (past + tsmp) // 4)
    l = 0

    w = w_in[l]
    w_packed = jnp.concatenate(
        [w[:, :4096], w[:, 4176:5200], w[:, 4096:4176], jnp.zeros((d, _IN_PACKED - 5200), w.dtype)],
        axis=1).astype(BF16)
    wpool_b = w_pool[l].astype(BF16)
    wout_b = w_out[l].astype(BF16)
    wg_b, wu_b, wd_b = w_gate[l].astype(BF16), w_up[l].astype(BF16), w_down[l].astype(BF16)

    rows = bp + bs
    c_all = jnp.concatenate([c_prompt, c_sample], axis=0)
    c_all = jnp.pad(c_all, ((0, -rows % 8), (0, 0)))
    mod = _ada(c_all, w_ada[l], b_ada[l])
    mods_p = [m[:bp, None, :] for m in jnp.split(mod, 6, axis=-1)]
    mods_s = [jnp.repeat(m[bp:rows], tsmp, axis=0)[None] for m in jnp.split(mod, 6, axis=-1)]
    mods_sb = [m[bp:rows, None, :] for m in jnp.split(mod, 6, axis=-1)]
    head_weights_t = lambda misc: jnp.swapaxes(misc[..., IDX_DIM:IDX_DIM + IDX_HEADS], 1, 2)

    q, k, v, qi, u, misc, kb, vb = _inproj(x_prompt, mods_p[0], mods_p[1], g_norm1[l], w_packed, tm=256)
    ts_p, sp_p = _dsa_tiles(tp, topk_p)
    assert sp_p == tp
    attn = _dsa(q, qi, head_weights_t(misc), misc[..., :IDX_DIM].astype(BF16), kb, _values_t(vb),
                pos0=0, topk=topk_p, tq=min(256, tp), ts=ts_p, tk=ts_p // 2)
    x1, h2 = _mix(x_prompt, attn, u, u, mods_p[2], mods_p[3], mods_p[4], g_norm2[l],
                  wpool_b, pool_scale[l], wout_b, tm=512, pos0=0, halo_from_u=True)
    y_prompt = _ffn(h2, x1, mods_p[5], g_final, wg_b, wu_b, wd_b, tm=512, tf=512)

    ntok = bs * tsmp
    xs_flat = x_sample.reshape(1, ntok, d)
    q_s, k_s, v_s, qi_s, u_s, misc_s, kb_s, vb_s = _inproj(
        xs_flat, mods_s[0], mods_s[1], g_norm1[l], w_packed, tm=min(256, ntok))
    per_batch = lambda a: a.reshape(bs, tsmp, a.shape[-1])
    q_s, k_s, v_s, u_s, misc_s, kb_s, vb_s = map(per_batch, (q_s, k_s, v_s, u_s, misc_s, kb_s, vb_s))
    qi_s = jnp.swapaxes(qi_s.reshape(IDX_HEADS, bs, tsmp, IDX_DIM), 0, 1)
    s_all = past + tsmp
    ts_s, s_pad = _dsa_tiles(s_all, topk_s)
    k_all = _pad_keys(jnp.concatenate([cache_k[l].reshape(bs, past, ATTN_WIDTH).astype(BF16), kb_s], axis=1), s_pad, 1)
    v_all = _pad_keys(jnp.concatenate([cache_v[l].reshape(bs, past, ATTN_WIDTH).astype(BF16), vb_s], axis=1), s_pad, 1)
    ki_all = _pad_keys(jnp.concatenate([cache_kidx[l], misc_s[..., :IDX_DIM]], axis=1).astype(BF16), s_pad, 1)
    attn_s = _dsa(q_s, qi_s, head_weights_t(misc_s), ki_all, k_all, _values_t(v_all),
                  pos0=past, topk=topk_s, tq=tsmp, ts=ts_s, tk=ts_s // 2)
    halo_s = jnp.pad(state_pool[l], ((0, 0), (HALO - POOL_PAD, 0), (0, 0)))
    x1_s, h2_s = _mix(x_sample, attn_s, u_s, halo_s, mods_sb[2], mods_sb[3], mods_sb[4], g_norm2[l],
                      wpool_b, pool_scale[l], wout_b, tm=tsmp, pos0=past, halo_from_u=False)
    y_sample = _ffn(h2_s.reshape(1, ntok, d), x1_s.reshape(1, ntok, d), mods_s[5], g_final,
                    wg_b, wu_b, wd_b, tm=min(256, ntok), tf=512).reshape(bs, tsmp, d)

    heads = lambda a: a.reshape(1, a.shape[0], a.shape[1], N_HEADS, HEAD_DIM)
    assert tp >= POOL_PAD and tsmp >= POOL_PAD
    pool_p = u[:, -POOL_PAD:]
    pool_s = u_s[:, -POOL_PAD:]
    return (y_prompt, y_sample,
            heads(k), heads(v), misc[..., :IDX_DIM][None], pool_p[None],
            heads(k_s), heads(v_s), misc_s[..., :IDX_DIM][None], pool_s[None])
```

```python
import functools

import numpy as np
import jax
import jax.numpy as jnp
from jax import lax
from jax.experimental import pallas as pl
from jax.experimental.pallas import tpu as pltpu

F32 = jnp.float32
BF16 = jnp.bfloat16

CHUNK = 64
N_HEADS = 8
HEAD_DIM = 128
ATTN_WIDTH = N_HEADS * HEAD_DIM
IDX_HEADS = 16
IDX_DIM = 64
TOPK_MAX = 256
POOL_WINDOWS = (2, 4, 8, 16)
POOL_GROUP_DIM = 256
POOL_WIDTH = POOL_GROUP_DIM * len(POOL_WINDOWS)
POOL_PAD = max(POOL_WINDOWS) - 1
HALO = 16
NEG_INF = -1e30
EPS = 1e-6
VALID_MIN = float(np.nextafter(np.float32(0.5) * np.float32(NEG_INF), np.float32(0.0)))
F32_MAX = float(np.finfo(np.float32).max)
MISC_WIDTH = 128
Q_PRESCALE = float(HEAD_DIM ** -0.5 * np.log2(np.e))
V_ROWS = HEAD_DIM + 16
SEARCH_MAX_ITERS = 24
SCORE_SUB_ROWS = 128
V7X_VMEM_BYTES = 64 * 1024 * 1024
VMEM_LIMIT = V7X_VMEM_BYTES - 8 * 1024 * 1024


def _params(sem):
    return pltpu.CompilerParams(dimension_semantics=sem, vmem_limit_bytes=VMEM_LIMIT)


def _resident(block_shape, index_map):
    return pl.BlockSpec(block_shape, index_map, pipeline_mode=pl.Buffered(1))


def _mod_spec(mod, tm):
    d = mod.shape[-1]
    if mod.shape[1] == 1:
        return pl.BlockSpec((1, 1, d), lambda b, i, *_: (b, 0, 0))
    return pl.BlockSpec((1, tm, d), lambda b, i, *_: (b, i, 0))


def _ada_kernel(c_ref, w_ref, b_ref, o_ref):
    c = c_ref[...]
    a = (c * jax.nn.sigmoid(c)).astype(BF16)
    o_ref[...] = jnp.dot(a, w_ref[...].astype(BF16), preferred_element_type=F32) + b_ref[...]


def _ada(c, w_ada, b_ada):
    rows, d = c.shape
    n = w_ada.shape[1]
    tn = 1536
    return pl.pallas_call(
        _ada_kernel,
        grid=(n // tn,),
        in_specs=[pl.BlockSpec((rows, d), lambda j: (0, 0)),
                  pl.BlockSpec((d, tn), lambda j: (0, j)),
                  pl.BlockSpec((1, tn), lambda j: (0, j))],
        out_specs=pl.BlockSpec((rows, tn), lambda j: (0, j)),
        out_shape=jax.ShapeDtypeStruct((rows, n), F32),
        compiler_params=_params(("arbitrary",)),
        name="ada",
    )(c, w_ada, b_ada.reshape(1, n))


_Q0, _K0, _V0, _QI0, _U0, _MISC0, _IN_PACKED = 0, 1024, 2048, 3072, 4096, 5120, 5248


def _rmsnorm_mod(x, g, scale, shift):
    y = x * lax.rsqrt(jnp.mean(x * x, axis=-1, keepdims=True) + EPS) * g
    return y * (1.0 + scale) + shift


def _inproj_kernel(x_ref, shift_ref, scale_ref, g_ref, w_ref,
                   q_ref, k_ref, v_ref, qi_ref, u_ref, misc_ref, kb_ref, vt_ref, kib_ref, wt_ref):
    h = _rmsnorm_mod(x_ref[0], g_ref[...], scale_ref[0], shift_ref[0]).astype(BF16)
    tm = h.shape[0]

    def proj(c0, width):
        return jnp.dot(h, w_ref[:, c0:c0 + width], preferred_element_type=F32)

    q_ref[0] = (proj(_Q0, ATTN_WIDTH) * Q_PRESCALE).astype(BF16)
    k = proj(_K0, ATTN_WIDTH)
    k_ref[0] = k
    kb_ref[0] = k.astype(BF16)
    v = proj(_V0, ATTN_WIDTH)
    v_ref[0] = v
    for hh in range(N_HEADS):
        r0 = hh * V_ROWS
        vt_ref[0, r0:r0 + HEAD_DIM, :] = v[:, hh * HEAD_DIM:(hh + 1) * HEAD_DIM].T.astype(BF16)
        vt_ref[0, r0 + HEAD_DIM:r0 + V_ROWS, :] = jnp.ones((V_ROWS - HEAD_DIM, tm), BF16)
    qi = proj(_QI0, IDX_HEADS * IDX_DIM).astype(BF16)
    for hh in range(IDX_HEADS):
        qi_ref[0, hh] = qi[:, hh * IDX_DIM:(hh + 1) * IDX_DIM]
    u_ref[0] = proj(_U0, POOL_WIDTH)
    misc = proj(_MISC0, MISC_WIDTH)
    misc_ref[0] = misc
    kib_ref[0] = misc[:, :IDX_DIM].astype(BF16)
    wt_ref[0] = misc.T[IDX_DIM:IDX_DIM + IDX_HEADS, :]


def _inproj(x, shift1, scale1, g1, w_packed, tm):
    bsz, t, d = x.shape
    tok = lambda width, dt: (jax.ShapeDtypeStruct((bsz, t, width), dt),
                             pl.BlockSpec((1, tm, width), lambda b, i: (b, i, 0)))
    outs = [tok(ATTN_WIDTH, BF16), tok(ATTN_WIDTH, F32), tok(ATTN_WIDTH, F32),
            (jax.ShapeDtypeStruct((bsz, IDX_HEADS, t, IDX_DIM), BF16),
             pl.BlockSpec((1, IDX_HEADS, tm, IDX_DIM), lambda b, i: (b, 0, i, 0))),
            tok(POOL_WIDTH, F32), tok(MISC_WIDTH, F32), tok(ATTN_WIDTH, BF16),
            (jax.ShapeDtypeStruct((bsz, N_HEADS * V_ROWS, t), BF16),
             pl.BlockSpec((1, N_HEADS * V_ROWS, tm), lambda b, i: (b, 0, i))),
            tok(IDX_DIM, BF16),
            (jax.ShapeDtypeStruct((bsz, IDX_HEADS, t), F32),
             pl.BlockSpec((1, IDX_HEADS, tm), lambda b, i: (b, 0, i)))]
    return pl.pallas_call(
        _inproj_kernel,
        grid=(bsz, t // tm),
        in_specs=[pl.BlockSpec((1, tm, d), lambda b, i: (b, i, 0)),
                  _mod_spec(shift1, tm), _mod_spec(scale1, tm),
                  pl.BlockSpec((1, d), lambda b, i: (0, 0)),
                  _resident((d, _IN_PACKED), lambda b, i: (0, 0))],
        out_specs=[o[1] for o in outs],
        out_shape=[o[0] for o in outs],
        compiler_params=_params(("arbitrary", "arbitrary")),
        name="inproj",
    )(x, shift1, scale1, g1.reshape(1, d), w_packed)


def _key_to_float(u):
    bits = jnp.where(u < 0, u & jnp.int32(0x7FFFFFFF), ~u)
    return lax.bitcast_convert_type(bits, F32)


def _dsa_kernel(q_ref, qi_ref, wt_ref, ki_ref, k_ref, vt_ref, o_ref,
                sc_ref, m_ref, acc_ref, s_ref, stat_ref, *, tq, ts, tk, pos0, topk):
    i = pl.program_id(1)
    s_pad = sc_ref.shape[0]
    sub_rows = min(ts, SCORE_SUB_ROWS)
    nt_dims = (((1,), (1,)), ((), ()))
    lim = jnp.minimum(((pos0 + (i + 1) * tq - 1) // CHUNK + 1) * CHUNK, s_pad)
    n_s = (lim + ts - 1) // ts
    qpos = pos0 + i * tq + lax.broadcasted_iota(jnp.int32, (1, tq), 1)
    klim = (qpos // CHUNK + 1) * CHUNK
    wv = wt_ref[0] * (IDX_HEADS ** -0.5 * IDX_DIM ** -0.5)

    def key_index(s0, rows):
        return s0 + lax.broadcasted_iota(jnp.int32, (rows, tq), 0)

    def score_tile(t, carry):
        smax, smin = carry
        for sub in range(ts // sub_rows):
            s0 = pl.multiple_of(t * ts + sub * sub_rows, sub_rows)
            kt = ki_ref[0, pl.ds(s0, sub_rows), :]
            acc = jnp.zeros((sub_rows, tq), F32)
            for hh in range(IDX_HEADS):
                d = lax.dot_general(kt, qi_ref[0, hh], nt_dims, preferred_element_type=F32)
                acc = acc + jnp.maximum(d, 0.0) * wv[hh:hh + 1, :]
            adm = key_index(s0, sub_rows) < klim
            sc_ref[pl.ds(s0, sub_rows), :] = jnp.where(adm, acc, NEG_INF)
            smax = jnp.maximum(smax, jnp.max(jnp.where(adm, acc, -F32_MAX), axis=0, keepdims=True))
            smin = jnp.minimum(smin, jnp.min(jnp.where(adm, acc, F32_MAX), axis=0, keepdims=True))
        return smax, smin

    smax, smin = lax.fori_loop(0, n_s, score_tile,
                               (jnp.full((1, tq), -F32_MAX, F32), jnp.full((1, tq), F32_MAX, F32)))

    def count(pred):
        def body(t, c):
            parts = []
            for sub in range(ts // sub_rows):
                s0 = pl.multiple_of(t * ts + sub * sub_rows, sub_rows)
                ind = jnp.where(pred(sc_ref[pl.ds(s0, sub_rows), :], s0), 1.0, 0.0)
                parts.append(jnp.sum(ind.reshape(sub_rows // 8, 8, tq), axis=0))
            return c + sum(parts)
        c = lax.fori_loop(0, n_s, body, jnp.zeros((8, tq), F32))
        return jnp.sum(c, axis=0, keepdims=True)

    def write_bias(sel_fn):
        def body(t, carry):
            for sub in range(ts // sub_rows):
                s0 = pl.multiple_of(t * ts + sub * sub_rows, sub_rows)
                x = sc_ref[pl.ds(s0, sub_rows), :]
                sc_ref[pl.ds(s0, sub_rows), :] = jnp.where(sel_fn(x, s0), 0.0, NEG_INF)
            return carry
        lax.fori_loop(0, n_s, body, 0)

        @pl.when(n_s * ts < s_pad)
        def _():
            sc_ref[pl.ds(pl.multiple_of(n_s * ts, ts), tk), :] = jnp.full((tk, tq), NEG_INF, F32)

    kf = float(topk)
    keep_all = klim <= topk

    def search_cond(st):
        it, _, _, _, _, _, done = st
        return jnp.logical_and(it < SEARCH_MAX_ITERS, jnp.min(done) < 0.5)

    def search_body(st):
        it, lo, hi, clo, chi, tau, done = st
        frac = (jnp.log(clo) - np.log(kf)) / (jnp.log(clo) - jnp.log(jnp.maximum(chi, 0.5)))
        frac = jnp.where(it % 3 == 2, 0.5, jnp.clip(frac, 0.02, 0.98))
        x = lo + frac * (hi - lo)
        c = count(lambda s, _: s >= x)
        hit = c == kf
        tau = jnp.where(jnp.logical_and(hit, done < 0.5), x, tau)
        done = jnp.where(hit, 1.0, done)
        above = c > kf
        lo = jnp.where(above, x, lo)
        clo = jnp.where(above, c, clo)
        hi = jnp.where(above, hi, x)
        chi = jnp.where(above, chi, c)
        return it + 1, lo, hi, clo, chi, tau, done

    n_adm = jnp.minimum(klim, s_pad).astype(F32)
    st0 = (jnp.int32(0), smin, smax, n_adm, jnp.ones((1, tq), F32),
           jnp.full((1, tq), VALID_MIN, F32), jnp.where(keep_all, 1.0, 0.0))
    _, _, _, _, _, tau_fast, done = lax.while_loop(search_cond, search_body, st0)
    converged = jnp.min(done) > 0.5

    @pl.when(converged)
    def _():
        tau_eff = jnp.maximum(tau_fast, VALID_MIN)
        write_bias(lambda x, _: x >= tau_eff)

    @pl.when(jnp.logical_not(converged))
    def _():
        def bit_body(b, carry):
            u, cnt_u = carry
            trial = u | lax.shift_left(jnp.int32(1), 31 - b)
            f = _key_to_float(trial)
            c = count(lambda s, _: s >= f)
            ok = c >= kf
            return jnp.where(ok, trial, u), jnp.where(ok, c, cnt_u)

        u, cnt_ge = lax.fori_loop(0, 32, bit_body,
                                  (jnp.zeros((1, tq), jnp.int32), jnp.zeros((1, tq), F32)))
        tau = _key_to_float(u)
        tie = jnp.logical_and(cnt_ge > kf, tau >= VALID_MIN)
        need = kf - count(lambda s, _: s > tau)
        nbits = s_pad.bit_length()

        def idx_body(b, lo):
            trial = lo | lax.shift_left(jnp.int32(1), (nbits - 1) - b)
            c = count(lambda s, s0: jnp.logical_and(s == tau, key_index(s0, s.shape[0]) < trial))
            return jnp.where(c < need, trial, lo)

        lo = lax.fori_loop(0, nbits, idx_body, jnp.zeros((1, tq), jnp.int32))
        jstar = jnp.where(tie, lo + 1, s_pad)

        def sel(s, s0):
            keep = jnp.logical_or(s > tau, jnp.logical_and(s == tau, key_index(s0, s.shape[0]) < jstar))
            return jnp.logical_and(keep, s >= VALID_MIN)
        write_bias(sel)

    m_ref[...] = jnp.full(m_ref.shape, NEG_INF, F32)
    acc_ref[...] = jnp.zeros(acc_ref.shape, F32)

    def logits(tile, hh, buf):
        k0 = pl.multiple_of(jnp.minimum(tile * tk, s_pad - tk), tk)
        hs = slice(hh * HEAD_DIM, (hh + 1) * HEAD_DIM)
        s = lax.dot_general(k_ref[0, pl.ds(k0, tk), hs], q_ref[0, :, hs], nt_dims,
                            preferred_element_type=F32) + sc_ref[pl.ds(k0, tk), :]
        s_ref[buf, hh] = s
        r = jnp.max(s.reshape(tk // 8, 8, tq), axis=0)
        for shift in (4, 2, 1):
            r = jnp.maximum(r, pltpu.roll(r, shift, 0))
        m_prev = m_ref[hh]
        m_new = jnp.maximum(m_prev, r)
        m_ref[hh] = m_new
        stat_ref[buf, 0, hh] = m_new
        stat_ref[buf, 1, hh] = jnp.exp2(m_prev - m_new)

    def accumulate(tile, hh, buf):
        k0 = pl.multiple_of(tile * tk, tk)
        vs = slice(hh * V_ROWS, (hh + 1) * V_ROWS)
        z = (s_ref[buf, hh].reshape(tk // 8, 8, tq) - stat_ref[buf, 0, hh]).reshape(tk, tq)
        pv = jnp.dot(vt_ref[0, vs, pl.ds(k0, tk)], jnp.exp2(z).astype(BF16), preferred_element_type=F32)
        acc = stat_ref[buf, 1, hh] * acc_ref[vs, :].reshape(V_ROWS // 8, 8, tq) + pv.reshape(V_ROWS // 8, 8, tq)
        acc_ref[vs, :] = acc.reshape(V_ROWS, tq)

    for hh in range(N_HEADS):
        logits(0, hh, 0)

    def pair_body(pr, carry):
        for hh in range(N_HEADS):
            logits(2 * pr + 1, hh, 1)
            accumulate(2 * pr, hh, 0)
        for hh in range(N_HEADS):
            logits(2 * pr + 2, hh, 0)
            accumulate(2 * pr + 1, hh, 1)
        return carry

    lax.fori_loop(0, n_s, pair_body, 0)
    for hh in range(N_HEADS):
        r0 = hh * V_ROWS
        inv_l = 1.0 / acc_ref[r0 + HEAD_DIM:r0 + HEAD_DIM + 1, :]
        o_ref[0, :, hh * HEAD_DIM:(hh + 1) * HEAD_DIM] = (
            acc_ref[r0:r0 + HEAD_DIM, :] * inv_l).T.astype(o_ref.dtype)


def _values_t(vb):
    bsz, s, _ = vb.shape
    vt = jnp.transpose(vb.reshape(bsz, s, N_HEADS, HEAD_DIM), (0, 2, 3, 1))
    ones = jnp.ones((bsz, N_HEADS, V_ROWS - HEAD_DIM, s), vb.dtype)
    return jnp.concatenate([vt, ones], axis=2).reshape(bsz, N_HEADS * V_ROWS, s)


def _dsa(q, qi, wt, kib, kb, vt, *, pos0, topk, tq, ts, tk):
    bsz, t, _ = q.shape
    s_pad = kb.shape[1]
    assert s_pad % ts == 0 and ts == 2 * tk and ts >= topk and t % tq == 0
    kern = functools.partial(_dsa_kernel, tq=tq, ts=ts, tk=tk, pos0=pos0, topk=topk)
    return pl.pallas_call(
        kern,
        grid=(bsz, t // tq),
        in_specs=[pl.BlockSpec((1, tq, ATTN_WIDTH), lambda b, i: (b, i, 0)),
                  pl.BlockSpec((1, IDX_HEADS, tq, IDX_DIM), lambda b, i: (b, 0, i, 0)),
                  pl.BlockSpec((1, IDX_HEADS, tq), lambda b, i: (b, 0, i)),
                  _resident((1, s_pad, IDX_DIM), lambda b, i: (b, 0, 0)),
                  _resident((1, s_pad, ATTN_WIDTH), lambda b, i: (b, 0, 0)),
                  _resident((1, N_HEADS * V_ROWS, s_pad), lambda b, i: (b, 0, 0))],
        out_specs=pl.BlockSpec((1, tq, ATTN_WIDTH), lambda b, i: (b, i, 0)),
        out_shape=jax.ShapeDtypeStruct((bsz, t, ATTN_WIDTH), BF16),
        scratch_shapes=[pltpu.VMEM((s_pad, tq), F32),
                        pltpu.VMEM((N_HEADS, 8, tq), F32),
                        pltpu.VMEM((N_HEADS * V_ROWS, tq), F32),
                        pltpu.VMEM((2, N_HEADS, tk, tq), F32),
                        pltpu.VMEM((2, 2, N_HEADS, 8, tq), F32)],
        compiler_params=_params(("arbitrary", "arbitrary")),
        name="dsa",
    )(q, qi, wt, kib, kb, vt)


def _mix_kernel(x_ref, attn_ref, u_ref, halo_ref, gate1_ref, shift2_ref, scale2_ref, g2_ref,
                wpool_ref, pscale_ref, wout_ref, x1_ref, h2_ref, xp_ref, *, tm, pos0, zero_first_halo):
    i = pl.program_id(1)
    halo = halo_ref[0]
    if zero_first_halo:
        halo = jnp.where(i == 0, 0.0, halo)
    xp_ref[0:HALO] = halo
    xp_ref[HALO:HALO + tm] = u_ref[0]
    pos = pos0 + i * tm + lax.broadcasted_iota(jnp.int32, (tm, 1), 0)
    mix = jnp.dot(attn_ref[0], wout_ref[0:ATTN_WIDTH], preferred_element_type=F32)
    for g, w in enumerate(POOL_WINDOWS):
        sl = slice(g * POOL_GROUP_DIM, (g + 1) * POOL_GROUP_DIM)
        cur = xp_ref[HALO:HALO + tm, sl]
        wsum = cur
        for dlt in range(1, w):
            wsum = wsum + xp_ref[HALO - dlt:HALO - dlt + tm, sl]
        inv_cnt = 1.0 / jnp.minimum(pos + 1, w).astype(F32)
        z = wsum * inv_cnt - cur
        y = jnp.dot(z.astype(BF16), wpool_ref[g], preferred_element_type=F32) * pscale_ref[:, sl]
        mix = mix + jnp.dot(y.astype(BF16), wout_ref[ATTN_WIDTH + g * POOL_GROUP_DIM:
                                                      ATTN_WIDTH + (g + 1) * POOL_GROUP_DIM],
                            preferred_element_type=F32)
    x1 = x_ref[0] + gate1_ref[0] * mix
    x1_ref[0] = x1
    h2_ref[0] = _rmsnorm_mod(x1, g2_ref[...], scale2_ref[0], shift2_ref[0]).astype(BF16)


def _mix(x, attn, u, halo, gate1, shift2, scale2, g2, wpool, pscale, wout, *, tm, pos0, halo_from_u):
    bsz, t, d = x.shape
    if halo_from_u:
        halo_spec = pl.BlockSpec((1, HALO, POOL_WIDTH),
                                 lambda b, i: (b, jnp.maximum(i * (tm // HALO) - 1, 0), 0))
    else:
        halo_spec = pl.BlockSpec((1, HALO, POOL_WIDTH), lambda b, i: (b, 0, 0))
    kern = functools.partial(_mix_kernel, tm=tm, pos0=pos0, zero_first_halo=halo_from_u)
    tokspec = lambda width: pl.BlockSpec((1, tm, width), lambda b, i: (b, i, 0))
    return pl.pallas_call(
        kern,
        grid=(bsz, t // tm),
        in_specs=[tokspec(d), tokspec(ATTN_WIDTH), tokspec(POOL_WIDTH), halo_spec,
                  _mod_spec(gate1, tm), _mod_spec(shift2, tm), _mod_spec(scale2, tm),
                  pl.BlockSpec((1, d), lambda b, i: (0, 0)),
                  _resident(wpool.shape, lambda b, i: (0, 0, 0)),
                  pl.BlockSpec((1, POOL_WIDTH), lambda b, i: (0, 0)),
                  _resident(wout.shape, lambda b, i: (0, 0))],
        out_specs=[tokspec(d), tokspec(d)],
        out_shape=[jax.ShapeDtypeStruct((bsz, t, d), F32), jax.ShapeDtypeStruct((bsz, t, d), BF16)],
        scratch_shapes=[pltpu.VMEM((HALO + tm, POOL_WIDTH), F32)],
        compiler_params=_params(("arbitrary", "arbitrary")),
        name="mix",
    )(x, attn, u, halo, gate1, shift2, scale2, g2.reshape(1, d), wpool, pscale.reshape(1, POOL_WIDTH), wout)


def _ffn_kernel(h_ref, x1_ref, gate2_ref, gf_ref, wg_ref, wu_ref, wd_ref, y_ref, acc_ref):
    j = pl.program_id(2)

    @pl.when(j == 0)
    def _():
        acc_ref[...] = jnp.zeros(acc_ref.shape, F32)

    h = h_ref[0]
    g = jnp.dot(h, wg_ref[...], preferred_element_type=F32)
    up = jnp.dot(h, wu_ref[...], preferred_element_type=F32)
    a = (g * jax.nn.sigmoid(g) * up).astype(BF16)
    acc_ref[...] += jnp.dot(a, wd_ref[...], preferred_element_type=F32)

    @pl.when(j == pl.num_programs(2) - 1)
    def _():
        out = x1_ref[0] + gate2_ref[0] * acc_ref[...]
        y_ref[0] = out * lax.rsqrt(jnp.mean(out * out, axis=-1, keepdims=True) + EPS) * gf_ref[...]


def _ffn(h2, x1, gate2, g_final, wg, wu, wd, *, tm, tf):
    bsz, t, d = x1.shape
    dff = wg.shape[1]
    tokspec = pl.BlockSpec((1, tm, d), lambda b, i, j: (b, i, 0))
    return pl.pallas_call(
        _ffn_kernel,
        grid=(bsz, t // tm, dff // tf),
        in_specs=[tokspec, tokspec, _mod_spec(gate2, tm),
                  pl.BlockSpec((1, d), lambda b, i, j: (0, 0)),
                  pl.BlockSpec((d, tf), lambda b, i, j: (0, j)),
                  pl.BlockSpec((d, tf), lambda b, i, j: (0, j)),
                  pl.BlockSpec((tf, d), lambda b, i, j: (j, 0))],
        out_specs=tokspec,
        out_shape=jax.ShapeDtypeStruct((bsz, t, d), F32),
        scratch_shapes=[pltpu.VMEM((tm, d), F32)],
        compiler_params=_params(("arbitrary", "arbitrary", "arbitrary")),
        name="ffn",
    )(h2, x1, gate2, g_final.reshape(1, d), wg, wu, wd)


DSA_KEY_TILE = 512


def _dsa_tiles(n_keys, topk):
    ts = max(min(DSA_KEY_TILE, n_keys), topk)
    ts = -(-ts // 256) * 256
    return ts, -(-n_keys // ts) * ts


def _pad_keys(a, s_pad, axis):
    pad = [(0, 0)] * a.ndim
    pad[axis] = (0, s_pad - a.shape[axis])
    return jnp.pad(a, pad)


def kernel(x_prompt, x_sample, c_prompt, c_sample, cache_k, cache_v, cache_kidx, state_pool,
           w_ada, b_ada, g_norm1, w_in, w_pool, pool_scale, w_out, g_norm2, w_gate, w_up, w_down, g_final):
    bp, tp, d = x_prompt.shape
    bs, tsmp, _ = x_sample.shape
    depth = w_ada.shape[0]
    past = cache_k.shape[2]
    assert depth == 1 and d == 2 * ATTN_WIDTH
    topk_p = min(TOPK_MAX, tp // 4)
    topk_s = min(TOPK_MAX, (past + tsmp) // 4)
    l = 0

    w = w_in[l]
    w_packed = jnp.concatenate(
        [w[:, :4096], w[:, 4176:5200], w[:, 4096:4176], jnp.zeros((d, _IN_PACKED - 5200), w.dtype)],
        axis=1).astype(BF16)
    wpool_b = w_pool[l].astype(BF16)
    wout_b = w_out[l].astype(BF16)
    wg_b, wu_b, wd_b = w_gate[l].astype(BF16), w_up[l].astype(BF16), w_down[l].astype(BF16)

    rows = bp + bs
    c_all = jnp.concatenate([c_prompt, c_sample], axis=0)
    c_all = jnp.pad(c_all, ((0, -rows % 8), (0, 0)))
    mod = _ada(c_all, w_ada[l], b_ada[l])
    mods_p = [m[:bp, None, :] for m in jnp.split(mod, 6, axis=-1)]
    mods_s = [jnp.repeat(m[bp:rows], tsmp, axis=0)[None] for m in jnp.split(mod, 6, axis=-1)]
    mods_sb = [m[bp:rows, None, :] for m in jnp.split(mod, 6, axis=-1)]

    q, k, v, qi, u, misc, kb, vt, kib, wt = _inproj(
        x_prompt, mods_p[0], mods_p[1], g_norm1[l], w_packed, tm=256)
    ts_p, sp_p = _dsa_tiles(tp, topk_p)
    assert sp_p == tp
    attn = _dsa(q, qi, wt, kib, kb, vt, pos0=0, topk=topk_p, tq=min(256, tp), ts=ts_p, tk=ts_p // 2)
    x1, h2 = _mix(x_prompt, attn, u, u, mods_p[2], mods_p[3], mods_p[4], g_norm2[l],
                  wpool_b, pool_scale[l], wout_b, tm=512, pos0=0, halo_from_u=True)
    y_prompt = _ffn(h2, x1, mods_p[5], g_final, wg_b, wu_b, wd_b, tm=512, tf=512)

    ntok = bs * tsmp
    xs_flat = x_sample.reshape(1, ntok, d)
    q_s, k_s, v_s, qi_s, u_s, misc_s, kb_s, _, kib_s, _ = _inproj(
        xs_flat, mods_s[0], mods_s[1], g_norm1[l], w_packed, tm=min(256, ntok))
    per_batch = lambda a: a.reshape(bs, tsmp, a.shape[-1])
    q_s, k_s, v_s, u_s, misc_s, kb_s, kib_s = map(per_batch, (q_s, k_s, v_s, u_s, misc_s, kb_s, kib_s))
    qi_s = jnp.swapaxes(qi_s.reshape(IDX_HEADS, bs, tsmp, IDX_DIM), 0, 1)
    s_all = past + tsmp
    ts_s, s_pad = _dsa_tiles(s_all, topk_s)
    cache_k2, cache_v2 = lax.optimization_barrier(
        (cache_k[l].reshape(bs, past, ATTN_WIDTH), cache_v[l].reshape(bs, past, ATTN_WIDTH)))
    k_all = _pad_keys(jnp.concatenate([cache_k2.astype(BF16), kb_s], axis=1), s_pad, 1)
    v_all = _pad_keys(jnp.concatenate([cache_v2.astype(BF16), v_s.astype(BF16)], axis=1), s_pad, 1)
    ki_all = _pad_keys(jnp.concatenate([cache_kidx[l].astype(BF16), kib_s], axis=1), s_pad, 1)
    wt_s = jnp.swapaxes(misc_s[..., IDX_DIM:IDX_DIM + IDX_HEADS], 1, 2)
    attn_s = _dsa(q_s, qi_s, wt_s, ki_all, k_all, _values_t(v_all),
                  pos0=past, topk=topk_s, tq=tsmp, ts=ts_s, tk=ts_s // 2)
    halo_s = jnp.pad(state_pool[l], ((0, 0), (HALO - POOL_PAD, 0), (0, 0)))
    x1_s, h2_s = _mix(x_sample, attn_s, u_s, halo_s, mods_sb[2], mods_sb[3], mods_sb[4], g_norm2[l],
                      wpool_b, pool_scale[l], wout_b, tm=tsmp, pos0=past, halo_from_u=False)
    y_sample = _ffn(h2_s.reshape(1, ntok, d), x1_s.reshape(1, ntok, d), mods_s[5], g_final,
                    wg_b, wu_b, wd_b, tm=min(256, ntok), tf=512).reshape(bs, tsmp, d)

    heads = lambda a: a.reshape(1, a.shape[0], a.shape[1], N_HEADS, HEAD_DIM)
    assert tp >= POOL_PAD and tsmp >= POOL_PAD
    pool_p = u[:, -POOL_PAD:]
    pool_s = u_s[:, -POOL_PAD:]
    return (y_prompt, y_sample,
            heads(k), heads(v), misc[..., :IDX_DIM][None], pool_p[None],
            heads(k_s), heads(v_s), misc_s[..., :IDX_DIM][None], pool_s[None])
```

```python
import functools

import numpy as np
import jax
import jax.numpy as jnp
from jax import lax
from jax.experimental import pallas as pl
from jax.experimental.pallas import tpu as pltpu

F32 = jnp.float32
BF16 = jnp.bfloat16

CHUNK = 64
N_HEADS = 8
HEAD_DIM = 128
ATTN_WIDTH = N_HEADS * HEAD_DIM
IDX_HEADS = 16
IDX_DIM = 64
TOPK_MAX = 256
POOL_WINDOWS = (2, 4, 8, 16)
POOL_GROUP_DIM = 256
POOL_WIDTH = POOL_GROUP_DIM * len(POOL_WINDOWS)
POOL_PAD = max(POOL_WINDOWS) - 1
HALO = 16
NEG_INF = -1e30
EPS = 1e-6
VALID_MIN = float(np.nextafter(np.float32(0.5) * np.float32(NEG_INF), np.float32(0.0)))
F32_MAX = float(np.finfo(np.float32).max)
MISC_WIDTH = 128
Q_PRESCALE = float(HEAD_DIM ** -0.5 * np.log2(np.e))
V_ROWS = HEAD_DIM + 16
SEARCH_MAX_ITERS = 24
SCORE_SUB_ROWS = 128
V7X_VMEM_BYTES = 64 * 1024 * 1024
VMEM_LIMIT = V7X_VMEM_BYTES - 8 * 1024 * 1024


def _params(sem):
    return pltpu.CompilerParams(dimension_semantics=sem, vmem_limit_bytes=VMEM_LIMIT)


def _resident(block_shape, index_map):
    return pl.BlockSpec(block_shape, index_map, pipeline_mode=pl.Buffered(1))


def _mod_spec(mod, tm):
    d = mod.shape[-1]
    if mod.shape[1] == 1:
        return pl.BlockSpec((1, 1, d), lambda b, i, *_: (b, 0, 0))
    return pl.BlockSpec((1, tm, d), lambda b, i, *_: (b, i, 0))


def _ada_kernel(c_ref, w_ref, b_ref, o_ref):
    c = c_ref[...]
    a = (c * jax.nn.sigmoid(c)).astype(BF16)
    o_ref[...] = jnp.dot(a, w_ref[...].astype(BF16), preferred_element_type=F32) + b_ref[...]


def _ada(c, w_ada, b_ada):
    rows, d = c.shape
    n = w_ada.shape[1]
    tn = 1536
    return pl.pallas_call(
        _ada_kernel,
        grid=(n // tn,),
        in_specs=[pl.BlockSpec((rows, d), lambda j: (0, 0)),
                  pl.BlockSpec((d, tn), lambda j: (0, j)),
                  pl.BlockSpec((1, tn), lambda j: (0, j))],
        out_specs=pl.BlockSpec((rows, tn), lambda j: (0, j)),
        out_shape=jax.ShapeDtypeStruct((rows, n), F32),
        compiler_params=_params(("arbitrary",)),
        name="ada",
    )(c, w_ada, b_ada.reshape(1, n))


_Q0, _K0, _V0, _QI0, _U0, _MISC0, _IN_PACKED = 0, 1024, 2048, 3072, 4096, 5120, 5248


def _rmsnorm_mod(x, g, scale, shift):
    y = x * lax.rsqrt(jnp.mean(x * x, axis=-1, keepdims=True) + EPS) * g
    return y * (1.0 + scale) + shift


def _inproj_kernel(x_ref, shift_ref, scale_ref, g_ref, w_ref,
                   q_ref, k_ref, v_ref, qi_ref, u_ref, misc_ref, kb_ref, vt_ref, kib_ref, wt_ref):
    h = _rmsnorm_mod(x_ref[0], g_ref[...], scale_ref[0], shift_ref[0]).astype(BF16)
    tm = h.shape[0]

    def proj(c0, width):
        return jnp.dot(h, w_ref[:, c0:c0 + width], preferred_element_type=F32)

    q_ref[0] = (proj(_Q0, ATTN_WIDTH) * Q_PRESCALE).astype(BF16)
    k = proj(_K0, ATTN_WIDTH)
    k_ref[0] = k
    kb_ref[0] = k.astype(BF16)
    v = proj(_V0, ATTN_WIDTH)
    v_ref[0] = v
    for hh in range(N_HEADS):
        r0 = hh * V_ROWS
        vt_ref[0, r0:r0 + HEAD_DIM, :] = v[:, hh * HEAD_DIM:(hh + 1) * HEAD_DIM].T.astype(BF16)
        vt_ref[0, r0 + HEAD_DIM:r0 + V_ROWS, :] = jnp.ones((V_ROWS - HEAD_DIM, tm), BF16)
    qi = proj(_QI0, IDX_HEADS * IDX_DIM).astype(BF16)
    for hh in range(IDX_HEADS):
        qi_ref[0, hh] = qi[:, hh * IDX_DIM:(hh + 1) * IDX_DIM]
    u_ref[0] = proj(_U0, POOL_WIDTH)
    misc = proj(_MISC0, MISC_WIDTH)
    misc_ref[0] = misc
    kib_ref[0] = misc[:, :IDX_DIM].astype(BF16)
    wt_ref[0] = misc.T[IDX_DIM:IDX_DIM + IDX_HEADS, :]


def _inproj(x, shift1, scale1, g1, w_packed, tm):
    bsz, t, d = x.shape
    tok = lambda width, dt: (jax.ShapeDtypeStruct((bsz, t, width), dt),
                             pl.BlockSpec((1, tm, width), lambda b, i: (b, i, 0)))
    outs = [tok(ATTN_WIDTH, BF16), tok(ATTN_WIDTH, F32), tok(ATTN_WIDTH, F32),
            (jax.ShapeDtypeStruct((bsz, IDX_HEADS, t, IDX_DIM), BF16),
             pl.BlockSpec((1, IDX_HEADS, tm, IDX_DIM), lambda b, i: (b, 0, i, 0))),
            tok(POOL_WIDTH, F32), tok(MISC_WIDTH, F32), tok(ATTN_WIDTH, BF16),
            (jax.ShapeDtypeStruct((bsz, N_HEADS * V_ROWS, t), BF16),
             pl.BlockSpec((1, N_HEADS * V_ROWS, tm), lambda b, i: (b, 0, i))),
            tok(IDX_DIM, BF16),
            (jax.ShapeDtypeStruct((bsz, IDX_HEADS, t), F32),
             pl.BlockSpec((1, IDX_HEADS, tm), lambda b, i: (b, 0, i)))]
    return pl.pallas_call(
        _inproj_kernel,
        grid=(bsz, t // tm),
        in_specs=[pl.BlockSpec((1, tm, d), lambda b, i: (b, i, 0)),
                  _mod_spec(shift1, tm), _mod_spec(scale1, tm),
                  pl.BlockSpec((1, d), lambda b, i: (0, 0)),
                  _resident((d, _IN_PACKED), lambda b, i: (0, 0))],
        out_specs=[o[1] for o in outs],
        out_shape=[o[0] for o in outs],
        compiler_params=_params(("arbitrary", "arbitrary")),
        name="inproj",
    )(x, shift1, scale1, g1.reshape(1, d), w_packed)


def _key_to_float(u):
    bits = jnp.where(u < 0, u & jnp.int32(0x7FFFFFFF), ~u)
    return lax.bitcast_convert_type(bits, F32)


def _dsa_kernel(q_ref, qi_ref, wt_ref, ki_ref, k_ref, vt_ref, o_ref,
                sc_ref, m_ref, acc_ref, s_ref, stat_ref, *, tq, ts, tk, pos0, topk):
    i = pl.program_id(1)
    s_pad = sc_ref.shape[0]
    sub_rows = min(ts, SCORE_SUB_ROWS)
    nt_dims = (((1,), (1,)), ((), ()))
    lim = jnp.minimum(((pos0 + (i + 1) * tq - 1) // CHUNK + 1) * CHUNK, s_pad)
    n_t = (lim + tk - 1) // tk
    n_s = n_t // 2
    has_tail = n_t % 2 == 1
    tail0 = pl.multiple_of(n_s * ts, tk)
    qpos = pos0 + i * tq + lax.broadcasted_iota(jnp.int32, (1, tq), 1)
    klim = (qpos // CHUNK + 1) * CHUNK
    wv = wt_ref[0] * (IDX_HEADS ** -0.5 * IDX_DIM ** -0.5)

    def key_index(s0, rows):
        return s0 + lax.broadcasted_iota(jnp.int32, (rows, tq), 0)

    def sub_starts(base, rows):
        return [pl.multiple_of(base + sub * sub_rows, sub_rows) for sub in range(rows // sub_rows)]

    def score_rows(base, rows, carry):
        smax, smin = carry
        for s0 in sub_starts(base, rows):
            kt = ki_ref[0, pl.ds(s0, sub_rows), :]
            acc = jnp.zeros((sub_rows, tq), F32)
            for hh in range(IDX_HEADS):
                d = lax.dot_general(kt, qi_ref[0, hh], nt_dims, preferred_element_type=F32)
                acc = acc + jnp.maximum(d, 0.0) * wv[hh:hh + 1, :]
            adm = key_index(s0, sub_rows) < klim
            sc_ref[pl.ds(s0, sub_rows), :] = jnp.where(adm, acc, NEG_INF)
            smax = jnp.maximum(smax, jnp.max(jnp.where(adm, acc, -F32_MAX), axis=0, keepdims=True))
            smin = jnp.minimum(smin, jnp.min(jnp.where(adm, acc, F32_MAX), axis=0, keepdims=True))
        return smax, smin

    extremes = lax.fori_loop(0, n_s, lambda t, c: score_rows(t * ts, ts, c),
                             (jnp.full((1, tq), -F32_MAX, F32), jnp.full((1, tq), F32_MAX, F32)))
    smax, smin = lax.cond(has_tail, lambda c: score_rows(tail0, tk, c), lambda c: c, extremes)

    def count(pred):
        def count_rows(base, rows, c):
            parts = []
            for s0 in sub_starts(base, rows):
                ind = jnp.where(pred(sc_ref[pl.ds(s0, sub_rows), :], s0), 1.0, 0.0)
                parts.append(jnp.sum(ind.reshape(sub_rows // 8, 8, tq), axis=0))
            return c + sum(parts)
        c = lax.fori_loop(0, n_s, lambda t, c: count_rows(t * ts, ts, c), jnp.zeros((8, tq), F32))
        c = lax.cond(has_tail, lambda c: count_rows(tail0, tk, c), lambda c: c, c)
        return jnp.sum(c, axis=0, keepdims=True)

    def write_bias(sel_fn):
        def bias_rows(base, rows):
            for s0 in sub_starts(base, rows):
                x = sc_ref[pl.ds(s0, sub_rows), :]
                sc_ref[pl.ds(s0, sub_rows), :] = jnp.where(sel_fn(x, s0), 0.0, NEG_INF)

        def body(t, carry):
            bias_rows(t * ts, ts)
            return carry
        lax.fori_loop(0, n_s, body, 0)

        @pl.when(has_tail)
        def _():
            bias_rows(tail0, tk)

        @pl.when(n_t * tk < s_pad)
        def _():
            sc_ref[pl.ds(pl.multiple_of(n_t * tk, tk), tk), :] = jnp.full((tk, tq), NEG_INF, F32)

    kf = float(topk)
    keep_all = klim <= topk

    def search_cond(st):
        it, _, _, _, _, _, done = st
        return jnp.logical_and(it < SEARCH_MAX_ITERS, jnp.min(done) < 0.5)

    def search_body(st):
        it, lo, hi, clo, chi, tau, done = st
        frac = (jnp.log(clo) - np.log(kf)) / (jnp.log(clo) - jnp.log(jnp.maximum(chi, 0.5)))
        frac = jnp.where(it % 3 == 2, 0.5, jnp.clip(frac, 0.02, 0.98))
        x = lo + frac * (hi - lo)
        c = count(lambda s, _: s >= x)
        hit = c == kf
        tau = jnp.where(jnp.logical_and(hit, done < 0.5), x, tau)
        done = jnp.where(hit, 1.0, done)
        above = c > kf
        lo = jnp.where(above, x, lo)
        clo = jnp.where(above, c, clo)
        hi = jnp.where(above, hi, x)
        chi = jnp.where(above, chi, c)
        return it + 1, lo, hi, clo, chi, tau, done

    n_adm = jnp.minimum(klim, s_pad).astype(F32)
    st0 = (jnp.int32(0), smin, smax, n_adm, jnp.ones((1, tq), F32),
           jnp.full((1, tq), VALID_MIN, F32), jnp.where(keep_all, 1.0, 0.0))
    _, _, _, _, _, tau_fast, done = lax.while_loop(search_cond, search_body, st0)
    converged = jnp.min(done) > 0.5

    @pl.when(converged)
    def _():
        tau_eff = jnp.maximum(tau_fast, VALID_MIN)
        write_bias(lambda x, _: x >= tau_eff)

    @pl.when(jnp.logical_not(converged))
    def _():
        def bit_body(b, carry):
            u, cnt_u = carry
            trial = u | lax.shift_left(jnp.int32(1), 31 - b)
            f = _key_to_float(trial)
            c = count(lambda s, _: s >= f)
            ok = c >= kf
            return jnp.where(ok, trial, u), jnp.where(ok, c, cnt_u)

        u, cnt_ge = lax.fori_loop(0, 32, bit_body,
                                  (jnp.zeros((1, tq), jnp.int32), jnp.zeros((1, tq), F32)))
        tau = _key_to_float(u)
        tie = jnp.logical_and(cnt_ge > kf, tau >= VALID_MIN)
        need = kf - count(lambda s, _: s > tau)
        nbits = s_pad.bit_length()

        def idx_body(b, lo):
            trial = lo | lax.shift_left(jnp.int32(1), (nbits - 1) - b)
            c = count(lambda s, s0: jnp.logical_and(s == tau, key_index(s0, s.shape[0]) < trial))
            return jnp.where(c < need, trial, lo)

        lo = lax.fori_loop(0, nbits, idx_body, jnp.zeros((1, tq), jnp.int32))
        jstar = jnp.where(tie, lo + 1, s_pad)

        def sel(s, s0):
            keep = jnp.logical_or(s > tau, jnp.logical_and(s == tau, key_index(s0, s.shape[0]) < jstar))
            return jnp.logical_and(keep, s >= VALID_MIN)
        write_bias(sel)

    m_ref[...] = jnp.full(m_ref.shape, NEG_INF, F32)
    acc_ref[...] = jnp.zeros(acc_ref.shape, F32)

    def logits(tile, hh, buf):
        k0 = pl.multiple_of(jnp.minimum(tile * tk, s_pad - tk), tk)
        hs = slice(hh * HEAD_DIM, (hh + 1) * HEAD_DIM)
        s = lax.dot_general(k_ref[0, pl.ds(k0, tk), hs], q_ref[0, :, hs], nt_dims,
                            preferred_element_type=F32) + sc_ref[pl.ds(k0, tk), :]
        s_ref[buf, hh] = s
        r = jnp.max(s.reshape(tk // 8, 8, tq), axis=0)
        for shift in (4, 2, 1):
            r = jnp.maximum(r, pltpu.roll(r, shift, 0))
        m_prev = m_ref[hh]
        m_new = jnp.maximum(m_prev, r)
        m_ref[hh] = m_new
        stat_ref[buf, 0, hh] = m_new
        stat_ref[buf, 1, hh] = jnp.exp2(m_prev - m_new)

    def accumulate(tile, hh, buf):
        k0 = pl.multiple_of(tile * tk, tk)
        vs = slice(hh * V_ROWS, (hh + 1) * V_ROWS)
        z = (s_ref[buf, hh].reshape(tk // 8, 8, tq) - stat_ref[buf, 0, hh]).reshape(tk, tq)
        pv = jnp.dot(vt_ref[0, vs, pl.ds(k0, tk)], jnp.exp2(z).astype(BF16), preferred_element_type=F32)
        acc = stat_ref[buf, 1, hh] * acc_ref[vs, :].reshape(V_ROWS // 8, 8, tq) + pv.reshape(V_ROWS // 8, 8, tq)
        acc_ref[vs, :] = acc.reshape(V_ROWS, tq)

    for hh in range(N_HEADS):
        logits(0, hh, 0)

    def pair_body(pr, carry):
        for hh in range(N_HEADS):
            logits(2 * pr + 1, hh, 1)
            accumulate(2 * pr, hh, 0)
        for hh in range(N_HEADS):
            logits(2 * pr + 2, hh, 0)
            accumulate(2 * pr + 1, hh, 1)
        return carry

    lax.fori_loop(0, n_s, pair_body, 0)

    @pl.when(has_tail)
    def _():
        for hh in range(N_HEADS):
            accumulate(n_t - 1, hh, 0)

    for hh in range(N_HEADS):
        r0 = hh * V_ROWS
        inv_l = 1.0 / acc_ref[r0 + HEAD_DIM:r0 + HEAD_DIM + 1, :]
        o_ref[0, :, hh * HEAD_DIM:(hh + 1) * HEAD_DIM] = (
            acc_ref[r0:r0 + HEAD_DIM, :] * inv_l).T.astype(o_ref.dtype)


def _values_t(vb):
    bsz, s, _ = vb.shape
    vt = jnp.transpose(vb.reshape(bsz, s, N_HEADS, HEAD_DIM), (0, 2, 3, 1))
    ones = jnp.ones((bsz, N_HEADS, V_ROWS - HEAD_DIM, s), vb.dtype)
    return jnp.concatenate([vt, ones], axis=2).reshape(bsz, N_HEADS * V_ROWS, s)


def _dsa(q, qi, wt, kib, kb, vt, *, pos0, topk, tq, ts, tk):
    bsz, t, _ = q.shape
    s_pad = kb.shape[1]
    assert s_pad % tk == 0 and ts == 2 * tk and tk >= topk and t % tq == 0
    kern = functools.partial(_dsa_kernel, tq=tq, ts=ts, tk=tk, pos0=pos0, topk=topk)
    return pl.pallas_call(
        kern,
        grid=(bsz, t // tq),
        in_specs=[pl.BlockSpec((1, tq, ATTN_WIDTH), lambda b, i: (b, i, 0)),
                  pl.BlockSpec((1, IDX_HEADS, tq, IDX_DIM), lambda b, i: (b, 0, i, 0)),
                  pl.BlockSpec((1, IDX_HEADS, tq), lambda b, i: (b, 0, i)),
                  _resident((1, s_pad, IDX_DIM), lambda b, i: (b, 0, 0)),
                  _resident((1, s_pad, ATTN_WIDTH), lambda b, i: (b, 0, 0)),
                  _resident((1, N_HEADS * V_ROWS, s_pad), lambda b, i: (b, 0, 0))],
        out_specs=pl.BlockSpec((1, tq, ATTN_WIDTH), lambda b, i: (b, i, 0)),
        out_shape=jax.ShapeDtypeStruct((bsz, t, ATTN_WIDTH), BF16),
        scratch_shapes=[pltpu.VMEM((s_pad, tq), F32),
                        pltpu.VMEM((N_HEADS, 8, tq), F32),
                        pltpu.VMEM((N_HEADS * V_ROWS, tq), F32),
                        pltpu.VMEM((2, N_HEADS, tk, tq), F32),
                        pltpu.VMEM((2, 2, N_HEADS, 8, tq), F32)],
        compiler_params=_params(("arbitrary", "arbitrary")),
        name="dsa",
    )(q, qi, wt, kib, kb, vt)


def _mix_kernel(x_ref, attn_ref, u_ref, halo_ref, gate1_ref, shift2_ref, scale2_ref, g2_ref,
                wpool_ref, pscale_ref, wout_ref, x1_ref, h2_ref, xp_ref, *, tm, pos0, zero_first_halo):
    i = pl.program_id(1)
    halo = halo_ref[0]
    if zero_first_halo:
        halo = jnp.where(i == 0, 0.0, halo)
    xp_ref[0:HALO] = halo
    xp_ref[HALO:HALO + tm] = u_ref[0]
    pos = pos0 + i * tm + lax.broadcasted_iota(jnp.int32, (tm, 1), 0)
    mix = jnp.dot(attn_ref[0], wout_ref[0:ATTN_WIDTH], preferred_element_type=F32)
    for g, w in enumerate(POOL_WINDOWS):
        sl = slice(g * POOL_GROUP_DIM, (g + 1) * POOL_GROUP_DIM)
        cur = xp_ref[HALO:HALO + tm, sl]
        wsum = cur
        for dlt in range(1, w):
            wsum = wsum + xp_ref[HALO - dlt:HALO - dlt + tm, sl]
        inv_cnt = 1.0 / jnp.minimum(pos + 1, w).astype(F32)
        z = wsum * inv_cnt - cur
        y = jnp.dot(z.astype(BF16), wpool_ref[g], preferred_element_type=F32) * pscale_ref[:, sl]
        mix = mix + jnp.dot(y.astype(BF16), wout_ref[ATTN_WIDTH + g * POOL_GROUP_DIM:
                                                      ATTN_WIDTH + (g + 1) * POOL_GROUP_DIM],
                            preferred_element_type=F32)
    x1 = x_ref[0] + gate1_ref[0] * mix
    x1_ref[0] = x1
    h2_ref[0] = _rmsnorm_mod(x1, g2_ref[...], scale2_ref[0], shift2_ref[0]).astype(BF16)


def _mix(x, attn, u, halo, gate1, shift2, scale2, g2, wpool, pscale, wout, *, tm, pos0, halo_from_u):
    bsz, t, d = x.shape
    if halo_from_u:
        halo_spec = pl.BlockSpec((1, HALO, POOL_WIDTH),
                                 lambda b, i: (b, jnp.maximum(i * (tm // HALO) - 1, 0), 0))
    else:
        halo_spec = pl.BlockSpec((1, HALO, POOL_WIDTH), lambda b, i: (b, 0, 0))
    kern = functools.partial(_mix_kernel, tm=tm, pos0=pos0, zero_first_halo=halo_from_u)
    tokspec = lambda width: pl.BlockSpec((1, tm, width), lambda b, i: (b, i, 0))
    return pl.pallas_call(
        kern,
        grid=(bsz, t // tm),
        in_specs=[tokspec(d), tokspec(ATTN_WIDTH), tokspec(POOL_WIDTH), halo_spec,
                  _mod_spec(gate1, tm), _mod_spec(shift2, tm), _mod_spec(scale2, tm),
                  pl.BlockSpec((1, d), lambda b, i: (0, 0)),
                  _resident(wpool.shape, lambda b, i: (0, 0, 0)),
                  pl.BlockSpec((1, POOL_WIDTH), lambda b, i: (0, 0)),
                  _resident(wout.shape, lambda b, i: (0, 0))],
        out_specs=[tokspec(d), tokspec(d)],
        out_shape=[jax.ShapeDtypeStruct((bsz, t, d), F32), jax.ShapeDtypeStruct((bsz, t, d), BF16)],
        scratch_shapes=[pltpu.VMEM((HALO + tm, POOL_WIDTH), F32)],
        compiler_params=_params(("arbitrary", "arbitrary")),
        name="mix",
    )(x, attn, u, halo, gate1, shift2, scale2, g2.reshape(1, d), wpool, pscale.reshape(1, POOL_WIDTH), wout)


def _ffn_kernel(h_ref, x1_ref, gate2_ref, gf_ref, wg_ref, wu_ref, wd_ref, y_ref, acc_ref):
    j = pl.program_id(2)

    @pl.when(j == 0)
    def _():
        acc_ref[...] = jnp.zeros(acc_ref.shape, F32)

    h = h_ref[0]
    g = jnp.dot(h, wg_ref[...], preferred_element_type=F32)
    up = jnp.dot(h, wu_ref[...], preferred_element_type=F32)
    a = (g * jax.nn.sigmoid(g) * up).astype(BF16)
    acc_ref[...] += jnp.dot(a, wd_ref[...], preferred_element_type=F32)

    @pl.when(j == pl.num_programs(2) - 1)
    def _():
        out = x1_ref[0] + gate2_ref[0] * acc_ref[...]
        y_ref[0] = out * lax.rsqrt(jnp.mean(out * out, axis=-1, keepdims=True) + EPS) * gf_ref[...]


def _ffn(h2, x1, gate2, g_final, wg, wu, wd, *, tm, tf):
    bsz, t, d = x1.shape
    dff = wg.shape[1]
    tokspec = pl.BlockSpec((1, tm, d), lambda b, i, j: (b, i, 0))
    return pl.pallas_call(
        _ffn_kernel,
        grid=(bsz, t // tm, dff // tf),
        in_specs=[tokspec, tokspec, _mod_spec(gate2, tm),
                  pl.BlockSpec((1, d), lambda b, i, j: (0, 0)),
                  pl.BlockSpec((d, tf), lambda b, i, j: (0, j)),
                  pl.BlockSpec((d, tf), lambda b, i, j: (0, j)),
                  pl.BlockSpec((tf, d), lambda b, i, j: (j, 0))],
        out_specs=tokspec,
        out_shape=jax.ShapeDtypeStruct((bsz, t, d), F32),
        scratch_shapes=[pltpu.VMEM((tm, d), F32)],
        compiler_params=_params(("arbitrary", "arbitrary", "arbitrary")),
        name="ffn",
    )(h2, x1, gate2, g_final.reshape(1, d), wg, wu, wd)


DSA_KEY_TILE = 512


def _dsa_tiles(n_keys, topk):
    ts = max(min(DSA_KEY_TILE, n_keys), 2 * topk)
    ts = -(-ts // 256) * 256
    return ts, -(-n_keys // (ts // 2)) * (ts // 2)


def _pad_keys(a, s_pad, axis):
    pad = [(0, 0)] * a.ndim
    pad[axis] = (0, s_pad - a.shape[axis])
    return jnp.pad(a, pad)


def kernel(x_prompt, x_sample, c_prompt, c_sample, cache_k, cache_v, cache_kidx, state_pool,
           w_ada, b_ada, g_norm1, w_in, w_pool, pool_scale, w_out, g_norm2, w_gate, w_up, w_down, g_final):
    bp, tp, d = x_prompt.shape
    bs, tsmp, _ = x_sample.shape
    depth = w_ada.shape[0]
    past = cache_k.shape[2]
    assert depth == 1 and d == 2 * ATTN_WIDTH
    topk_p = min(TOPK_MAX, tp // 4)
    topk_s = min(TOPK_MAX, (past + tsmp) // 4)
    l = 0

    w = w_in[l]
    w_packed = jnp.concatenate(
        [w[:, :4096], w[:, 4176:5200], w[:, 4096:4176], jnp.zeros((d, _IN_PACKED - 5200), w.dtype)],
        axis=1).astype(BF16)
    wpool_b = w_pool[l].astype(BF16)
    wout_b = w_out[l].astype(BF16)
    wg_b, wu_b, wd_b = w_gate[l].astype(BF16), w_up[l].astype(BF16), w_down[l].astype(BF16)

    rows = bp + bs
    c_all = jnp.concatenate([c_prompt, c_sample], axis=0)
    c_all = jnp.pad(c_all, ((0, -rows % 8), (0, 0)))
    mod = _ada(c_all, w_ada[l], b_ada[l])
    mods_p = [m[:bp, None, :] for m in jnp.split(mod, 6, axis=-1)]
    mods_s = [jnp.repeat(m[bp:rows], tsmp, axis=0)[None] for m in jnp.split(mod, 6, axis=-1)]
    mods_sb = [m[bp:rows, None, :] for m in jnp.split(mod, 6, axis=-1)]

    q, k, v, qi, u, misc, kb, vt, kib, wt = _inproj(
        x_prompt, mods_p[0], mods_p[1], g_norm1[l], w_packed, tm=256)
    ts_p, sp_p = _dsa_tiles(tp, topk_p)
    assert sp_p == tp
    attn = _dsa(q, qi, wt, kib, kb, vt, pos0=0, topk=topk_p, tq=min(256, tp), ts=ts_p, tk=ts_p // 2)
    x1, h2 = _mix(x_prompt, attn, u, u, mods_p[2], mods_p[3], mods_p[4], g_norm2[l],
                  wpool_b, pool_scale[l], wout_b, tm=512, pos0=0, halo_from_u=True)
    y_prompt = _ffn(h2, x1, mods_p[5], g_final, wg_b, wu_b, wd_b, tm=512, tf=512)

    ntok = bs * tsmp
    xs_flat = x_sample.reshape(1, ntok, d)
    q_s, k_s, v_s, qi_s, u_s, misc_s, kb_s, _, kib_s, _ = _inproj(
        xs_flat, mods_s[0], mods_s[1], g_norm1[l], w_packed, tm=min(256, ntok))
    per_batch = lambda a: a.reshape(bs, tsmp, a.shape[-1])
    q_s, k_s, v_s, u_s, misc_s, kb_s, kib_s = map(per_batch, (q_s, k_s, v_s, u_s, misc_s, kb_s, kib_s))
    qi_s = jnp.swapaxes(qi_s.reshape(IDX_HEADS, bs, tsmp, IDX_DIM), 0, 1)
    s_all = past + tsmp
    ts_s, s_pad = _dsa_tiles(s_all, topk_s)
    cache_k2, cache_v2 = lax.optimization_barrier(
        (cache_k[l].reshape(bs, past, ATTN_WIDTH), cache_v[l].reshape(bs, past, ATTN_WIDTH)))
    k_all = _pad_keys(jnp.concatenate([cache_k2.astype(BF16), kb_s], axis=1), s_pad, 1)
    v_all = _pad_keys(jnp.concatenate([cache_v2.astype(BF16), v_s.astype(BF16)], axis=1), s_pad, 1)
    ki_all = _pad_keys(jnp.concatenate([cache_kidx[l].astype(BF16), kib_s], axis=1), s_pad, 1)
    wt_s = jnp.swapaxes(misc_s[..., IDX_DIM:IDX_DIM + IDX_HEADS], 1, 2)
    attn_s = _dsa(q_s, qi_s, wt_s, ki_all, k_all, _values_t(v_all),
                  pos0=past, topk=topk_s, tq=tsmp, ts=ts_s, tk=ts_s // 2)
    halo_s = jnp.pad(state_pool[l], ((0, 0), (HALO - POOL_PAD, 0), (0, 0)))
    x1_s, h2_s = _mix(x_sample, attn_s, u_s, halo_s, mods_sb[2], mods_sb[3], mods_sb[4], g_norm2[l],
                      wpool_b, pool_scale[l], wout_b, tm=tsmp, pos0=past, halo_from_u=False)
    y_sample = _ffn(h2_s.reshape(1, ntok, d), x1_s.reshape(1, ntok, d), mods_s[5], g_final,
                    wg_b, wu_b, wd_b, tm=min(256, ntok), tf=512).reshape(bs, tsmp, d)

    heads = lambda a: a.reshape(1, a.shape[0], a.shape[1], N_HEADS, HEAD_DIM)
    assert tp >= POOL_PAD and tsmp >= POOL_PAD
    pool_p = u[:, -POOL_PAD:]
    pool_s = u_s[:, -POOL_PAD:]
    return (y_prompt, y_sample,
            heads(k), heads(v), misc[..., :IDX_DIM][None], pool_p[None],
            heads(k_s), heads(v_s), misc_s[..., :IDX_DIM][None], pool_s[None])
```

```python
import functools

import numpy as np
import jax
import jax.numpy as jnp
from jax import lax
from jax.experimental import pallas as pl
from jax.experimental.pallas import tpu as pltpu

F32 = jnp.float32
BF16 = jnp.bfloat16

CHUNK = 64
N_HEADS = 8
HEAD_DIM = 128
ATTN_WIDTH = N_HEADS * HEAD_DIM
IDX_HEADS = 16
IDX_DIM = 64
TOPK_MAX = 256
POOL_WINDOWS = (2, 4, 8, 16)
POOL_GROUP_DIM = 256
POOL_WIDTH = POOL_GROUP_DIM * len(POOL_WINDOWS)
POOL_PAD = max(POOL_WINDOWS) - 1
HALO = 16
NEG_INF = -1e30
EPS = 1e-6
VALID_MIN = float(np.nextafter(np.float32(0.5) * np.float32(NEG_INF), np.float32(0.0)))
F32_MAX = float(np.finfo(np.float32).max)
MISC_WIDTH = 128
Q_PRESCALE = float(HEAD_DIM ** -0.5 * np.log2(np.e))
V_ROWS = HEAD_DIM + 16
SEARCH_MAX_ITERS = 28
SEARCH_FIXED_ITERS = 14
SCORE_SUB_ROWS = 128
V7X_VMEM_BYTES = 64 * 1024 * 1024
VMEM_LIMIT = V7X_VMEM_BYTES - 8 * 1024 * 1024


def _params(sem):
    return pltpu.CompilerParams(dimension_semantics=sem, vmem_limit_bytes=VMEM_LIMIT)


def _resident(block_shape, index_map):
    return pl.BlockSpec(block_shape, index_map, pipeline_mode=pl.Buffered(1))


def _mod_spec(mod, tm):
    d = mod.shape[-1]
    if mod.shape[1] == 1:
        return pl.BlockSpec((1, 1, d), lambda b, i, *_: (b, 0, 0))
    return pl.BlockSpec((1, tm, d), lambda b, i, *_: (b, i, 0))


def _ada_kernel(c_ref, w_ref, b_ref, o_ref):
    c = c_ref[...]
    a = (c * jax.nn.sigmoid(c)).astype(BF16)
    o_ref[...] = jnp.dot(a, w_ref[...].astype(BF16), preferred_element_type=F32) + b_ref[...]


def _ada(c, w_ada, b_ada):
    rows, d = c.shape
    n = w_ada.shape[1]
    tn = 1536
    return pl.pallas_call(
        _ada_kernel,
        grid=(n // tn,),
        in_specs=[pl.BlockSpec((rows, d), lambda j: (0, 0)),
                  pl.BlockSpec((d, tn), lambda j: (0, j)),
                  pl.BlockSpec((1, tn), lambda j: (0, j))],
        out_specs=pl.BlockSpec((rows, tn), lambda j: (0, j)),
        out_shape=jax.ShapeDtypeStruct((rows, n), F32),
        compiler_params=_params(("arbitrary",)),
        name="ada",
    )(c, w_ada, b_ada.reshape(1, n))


_Q0, _K0, _V0, _QI0, _U0, _MISC0, _IN_PACKED = 0, 1024, 2048, 3072, 4096, 5120, 5248


def _rmsnorm_mod(x, g, scale, shift):
    y = x * lax.rsqrt(jnp.mean(x * x, axis=-1, keepdims=True) + EPS) * g
    return y * (1.0 + scale) + shift


def _inproj_kernel(x_ref, shift_ref, scale_ref, g_ref, w_ref,
                   q_ref, k_ref, v_ref, qi_ref, u_ref, misc_ref, kb_ref, vt_ref, kib_ref, wt_ref):
    h = _rmsnorm_mod(x_ref[0], g_ref[...], scale_ref[0], shift_ref[0]).astype(BF16)
    tm = h.shape[0]

    def proj(c0, width):
        return jnp.dot(h, w_ref[:, c0:c0 + width], preferred_element_type=F32)

    q_ref[0] = (proj(_Q0, ATTN_WIDTH) * Q_PRESCALE).astype(BF16)
    k = proj(_K0, ATTN_WIDTH)
    k_ref[0] = k
    kb_ref[0] = k.astype(BF16)
    v = proj(_V0, ATTN_WIDTH)
    v_ref[0] = v
    for hh in range(N_HEADS):
        r0 = hh * V_ROWS
        vt_ref[0, r0:r0 + HEAD_DIM, :] = v[:, hh * HEAD_DIM:(hh + 1) * HEAD_DIM].T.astype(BF16)
        vt_ref[0, r0 + HEAD_DIM:r0 + V_ROWS, :] = jnp.ones((V_ROWS - HEAD_DIM, tm), BF16)
    qi = proj(_QI0, IDX_HEADS * IDX_DIM).astype(BF16)
    for hh in range(IDX_HEADS):
        qi_ref[0, hh] = qi[:, hh * IDX_DIM:(hh + 1) * IDX_DIM]
    u_ref[0] = proj(_U0, POOL_WIDTH)
    misc = proj(_MISC0, MISC_WIDTH)
    misc_ref[0] = misc
    kib_ref[0] = misc[:, :IDX_DIM].astype(BF16)
    wt_ref[0] = misc.T[IDX_DIM:IDX_DIM + IDX_HEADS, :]


def _inproj(x, shift1, scale1, g1, w_packed, tm):
    bsz, t, d = x.shape
    tok = lambda width, dt: (jax.ShapeDtypeStruct((bsz, t, width), dt),
                             pl.BlockSpec((1, tm, width), lambda b, i: (b, i, 0)))
    outs = [tok(ATTN_WIDTH, BF16), tok(ATTN_WIDTH, F32), tok(ATTN_WIDTH, F32),
            (jax.ShapeDtypeStruct((bsz, IDX_HEADS, t, IDX_DIM), BF16),
             pl.BlockSpec((1, IDX_HEADS, tm, IDX_DIM), lambda b, i: (b, 0, i, 0))),
            tok(POOL_WIDTH, F32), tok(MISC_WIDTH, F32), tok(ATTN_WIDTH, BF16),
            (jax.ShapeDtypeStruct((bsz, N_HEADS * V_ROWS, t), BF16),
             pl.BlockSpec((1, N_HEADS * V_ROWS, tm), lambda b, i: (b, 0, i))),
            tok(IDX_DIM, BF16),
            (jax.ShapeDtypeStruct((bsz, IDX_HEADS, t), F32),
             pl.BlockSpec((1, IDX_HEADS, tm), lambda b, i: (b, 0, i)))]
    return pl.pallas_call(
        _inproj_kernel,
        grid=(bsz, t // tm),
        in_specs=[pl.BlockSpec((1, tm, d), lambda b, i: (b, i, 0)),
                  _mod_spec(shift1, tm), _mod_spec(scale1, tm),
                  pl.BlockSpec((1, d), lambda b, i: (0, 0)),
                  _resident((d, _IN_PACKED), lambda b, i: (0, 0))],
        out_specs=[o[1] for o in outs],
        out_shape=[o[0] for o in outs],
        compiler_params=_params(("arbitrary", "arbitrary")),
        name="inproj",
    )(x, shift1, scale1, g1.reshape(1, d), w_packed)


def _key_to_float(u):
    bits = jnp.where(u < 0, u & jnp.int32(0x7FFFFFFF), ~u)
    return lax.bitcast_convert_type(bits, F32)


def _dsa_kernel(q_ref, qi_ref, wt_ref, ki_ref, k_ref, vt_ref, o_ref,
                sc_ref, m_ref, acc_ref, s_ref, stat_ref, *, tq, ts, tk, pos0, topk):
    i = pl.program_id(1)
    s_pad = sc_ref.shape[0]
    sub_rows = min(ts, SCORE_SUB_ROWS)
    nt_dims = (((1,), (1,)), ((), ()))
    lim = jnp.minimum(((pos0 + (i + 1) * tq - 1) // CHUNK + 1) * CHUNK, s_pad)
    n_t = (lim + tk - 1) // tk
    n_s = n_t // 2
    has_tail = n_t % 2 == 1
    tail0 = pl.multiple_of(n_s * ts, tk)
    qpos = pos0 + i * tq + lax.broadcasted_iota(jnp.int32, (1, tq), 1)
    klim = (qpos // CHUNK + 1) * CHUNK
    wv = wt_ref[0] * (IDX_HEADS ** -0.5 * IDX_DIM ** -0.5)

    def key_index(s0, rows):
        return s0 + lax.broadcasted_iota(jnp.int32, (rows, tq), 0)

    def sub_starts(base, rows):
        return [pl.multiple_of(base + sub * sub_rows, sub_rows) for sub in range(rows // sub_rows)]

    def score_rows(base, rows, carry):
        smax, smin = carry
        for s0 in sub_starts(base, rows):
            kt = ki_ref[0, pl.ds(s0, sub_rows), :]
            acc = jnp.zeros((sub_rows, tq), F32)
            for hh in range(IDX_HEADS):
                d = lax.dot_general(kt, qi_ref[0, hh], nt_dims, preferred_element_type=F32)
                acc = acc + jnp.maximum(d, 0.0) * wv[hh:hh + 1, :]
            adm = key_index(s0, sub_rows) < klim
            sc_ref[pl.ds(s0, sub_rows), :] = jnp.where(adm, acc, NEG_INF)
            smax = jnp.maximum(smax, jnp.max(jnp.where(adm, acc, -F32_MAX), axis=0, keepdims=True))
            smin = jnp.minimum(smin, jnp.min(jnp.where(adm, acc, F32_MAX), axis=0, keepdims=True))
        return smax, smin

    extremes = lax.fori_loop(0, n_s, lambda t, c: score_rows(t * ts, ts, c),
                             (jnp.full((1, tq), -F32_MAX, F32), jnp.full((1, tq), F32_MAX, F32)))
    smax, smin = lax.cond(has_tail, lambda c: score_rows(tail0, tk, c), lambda c: c, extremes)

    def count(pred):
        def count_rows(base, rows, c):
            parts = []
            for s0 in sub_starts(base, rows):
                ind = jnp.where(pred(sc_ref[pl.ds(s0, sub_rows), :], s0), 1.0, 0.0)
                parts.append(jnp.sum(ind.reshape(sub_rows // 8, 8, tq), axis=0))
            return c + sum(parts)
        c = lax.fori_loop(0, n_s, lambda t, c: count_rows(t * ts, ts, c), jnp.zeros((8, tq), F32))
        c = lax.cond(has_tail, lambda c: count_rows(tail0, tk, c), lambda c: c, c)
        return jnp.sum(c, axis=0, keepdims=True)

    def write_bias(sel_fn):
        def bias_rows(base, rows):
            for s0 in sub_starts(base, rows):
                x = sc_ref[pl.ds(s0, sub_rows), :]
                sc_ref[pl.ds(s0, sub_rows), :] = jnp.where(sel_fn(x, s0), 0.0, NEG_INF)

        def body(t, carry):
            bias_rows(t * ts, ts)
            return carry
        lax.fori_loop(0, n_s, body, 0)

        @pl.when(has_tail)
        def _():
            bias_rows(tail0, tk)

        @pl.when(n_t * tk < s_pad)
        def _():
            sc_ref[pl.ds(pl.multiple_of(n_t * tk, tk), tk), :] = jnp.full((tk, tq), NEG_INF, F32)

    kf = float(topk)
    keep_all = klim <= topk

    def search_cond(st):
        it, _, _, _, _, _, done = st
        return jnp.logical_and(it < SEARCH_MAX_ITERS, jnp.min(done) < 0.5)

    def search_body(st):
        it, lo, hi, clo, chi, tau, done = st
        frac = (jnp.log(clo) - np.log(kf)) / (jnp.log(clo) - jnp.log(jnp.maximum(chi, 0.5)))
        frac = jnp.where(it % 3 == 2, 0.5, jnp.clip(frac, 0.02, 0.98))
        x = lo + frac * (hi - lo)
        c = count(lambda s, _: s >= x)
        hit = c == kf
        tau = jnp.where(jnp.logical_and(hit, done < 0.5), x, tau)
        done = jnp.where(hit, 1.0, done)
        above = c > kf
        lo = jnp.where(above, x, lo)
        clo = jnp.where(above, c, clo)
        hi = jnp.where(above, hi, x)
        chi = jnp.where(above, chi, c)
        return it + 1, lo, hi, clo, chi, tau, done

    n_adm = jnp.minimum(klim, s_pad).astype(F32)
    st0 = (jnp.int32(0), smin, smax, n_adm, jnp.ones((1, tq), F32),
           jnp.full((1, tq), VALID_MIN, F32), jnp.where(keep_all, 1.0, 0.0))
    st = lax.fori_loop(0, SEARCH_FIXED_ITERS, lambda _, st: search_body(st), st0)
    _, _, _, _, _, tau_fast, done = lax.while_loop(search_cond, search_body, st)
    converged = jnp.min(done) > 0.5

    @pl.when(converged)
    def _():
        tau_eff = jnp.maximum(tau_fast, VALID_MIN)
        write_bias(lambda x, _: x >= tau_eff)

    @pl.when(jnp.logical_not(converged))
    def _():
        def bit_body(b, carry):
            u, cnt_u = carry
            trial = u | lax.shift_left(jnp.int32(1), 31 - b)
            f = _key_to_float(trial)
            c = count(lambda s, _: s >= f)
            ok = c >= kf
            return jnp.where(ok, trial, u), jnp.where(ok, c, cnt_u)

        u, cnt_ge = lax.fori_loop(0, 32, bit_body,
                                  (jnp.zeros((1, tq), jnp.int32), jnp.zeros((1, tq), F32)))
        tau = _key_to_float(u)
        tie = jnp.logical_and(cnt_ge > kf, tau >= VALID_MIN)
        need = kf - count(lambda s, _: s > tau)
        nbits = s_pad.bit_length()

        def idx_body(b, lo):
            trial = lo | lax.shift_left(jnp.int32(1), (nbits - 1) - b)
            c = count(lambda s, s0: jnp.logical_and(s == tau, key_index(s0, s.shape[0]) < trial))
            return jnp.where(c < need, trial, lo)

        lo = lax.fori_loop(0, nbits, idx_body, jnp.zeros((1, tq), jnp.int32))
        jstar = jnp.where(tie, lo + 1, s_pad)

        def sel(s, s0):
            keep = jnp.logical_or(s > tau, jnp.logical_and(s == tau, key_index(s0, s.shape[0]) < jstar))
            return jnp.logical_and(keep, s >= VALID_MIN)
        write_bias(sel)

    m_ref[...] = jnp.full(m_ref.shape, NEG_INF, F32)
    acc_ref[...] = jnp.zeros(acc_ref.shape, F32)

    def logits(tile, hh, buf):
        k0 = pl.multiple_of(jnp.minimum(tile * tk, s_pad - tk), tk)
        hs = slice(hh * HEAD_DIM, (hh + 1) * HEAD_DIM)
        s = lax.dot_general(k_ref[0, pl.ds(k0, tk), hs], q_ref[0, :, hs], nt_dims,
                            preferred_element_type=F32) + sc_ref[pl.ds(k0, tk), :]
        s_ref[buf, hh] = s
        r = jnp.max(s.reshape(tk // 8, 8, tq), axis=0)
        for shift in (4, 2, 1):
            r = jnp.maximum(r, pltpu.roll(r, shift, 0))
        m_prev = m_ref[hh]
        m_new = jnp.maximum(m_prev, r)
        m_ref[hh] = m_new
        stat_ref[buf, 0, hh] = m_new
        stat_ref[buf, 1, hh] = jnp.exp2(m_prev - m_new)

    def accumulate(tile, hh, buf):
        k0 = pl.multiple_of(tile * tk, tk)
        vs = slice(hh * V_ROWS, (hh + 1) * V_ROWS)
        z = (s_ref[buf, hh].reshape(tk // 8, 8, tq) - stat_ref[buf, 0, hh]).reshape(tk, tq)
        pv = jnp.dot(vt_ref[0, vs, pl.ds(k0, tk)], jnp.exp2(z).astype(BF16), preferred_element_type=F32)
        acc = stat_ref[buf, 1, hh] * acc_ref[vs, :].reshape(V_ROWS // 8, 8, tq) + pv.reshape(V_ROWS // 8, 8, tq)
        acc_ref[vs, :] = acc.reshape(V_ROWS, tq)

    for hh in range(N_HEADS):
        logits(0, hh, 0)

    def pair_body(pr, carry):
        for hh in range(N_HEADS):
            logits(2 * pr + 1, hh, 1)
            accumulate(2 * pr, hh, 0)
        for hh in range(N_HEADS):
            logits(2 * pr + 2, hh, 0)
            accumulate(2 * pr + 1, hh, 1)
        return carry

    lax.fori_loop(0, n_s, pair_body, 0)

    @pl.when(has_tail)
    def _():
        for hh in range(N_HEADS):
            accumulate(n_t - 1, hh, 0)

    for hh in range(N_HEADS):
        r0 = hh * V_ROWS
        inv_l = 1.0 / acc_ref[r0 + HEAD_DIM:r0 + HEAD_DIM + 1, :]
        o_ref[0, :, hh * HEAD_DIM:(hh + 1) * HEAD_DIM] = (
            acc_ref[r0:r0 + HEAD_DIM, :] * inv_l).T.astype(o_ref.dtype)


def _values_t(vb):
    bsz, s, _ = vb.shape
    vt = jnp.transpose(vb.reshape(bsz, s, N_HEADS, HEAD_DIM), (0, 2, 3, 1))
    ones = jnp.ones((bsz, N_HEADS, V_ROWS - HEAD_DIM, s), vb.dtype)
    return jnp.concatenate([vt, ones], axis=2).reshape(bsz, N_HEADS * V_ROWS, s)


def _dsa(q, qi, wt, kib, kb, vt, *, pos0, topk, tq, ts, tk):
    bsz, t, _ = q.shape
    s_pad = kb.shape[1]
    assert s_pad % tk == 0 and ts == 2 * tk and tk >= topk and t % tq == 0
    kern = functools.partial(_dsa_kernel, tq=tq, ts=ts, tk=tk, pos0=pos0, topk=topk)
    return pl.pallas_call(
        kern,
        grid=(bsz, t // tq),
        in_specs=[pl.BlockSpec((1, tq, ATTN_WIDTH), lambda b, i: (b, i, 0)),
                  pl.BlockSpec((1, IDX_HEADS, tq, IDX_DIM), lambda b, i: (b, 0, i, 0)),
                  pl.BlockSpec((1, IDX_HEADS, tq), lambda b, i: (b, 0, i)),
                  _resident((1, s_pad, IDX_DIM), lambda b, i: (b, 0, 0)),
                  _resident((1, s_pad, ATTN_WIDTH), lambda b, i: (b, 0, 0)),
                  _resident((1, N_HEADS * V_ROWS, s_pad), lambda b, i: (b, 0, 0))],
        out_specs=pl.BlockSpec((1, tq, ATTN_WIDTH), lambda b, i: (b, i, 0)),
        out_shape=jax.ShapeDtypeStruct((bsz, t, ATTN_WIDTH), BF16),
        scratch_shapes=[pltpu.VMEM((s_pad, tq), F32),
                        pltpu.VMEM((N_HEADS, 8, tq), F32),
                        pltpu.VMEM((N_HEADS * V_ROWS, tq), F32),
                        pltpu.VMEM((2, N_HEADS, tk, tq), F32),
                        pltpu.VMEM((2, 2, N_HEADS, 8, tq), F32)],
        compiler_params=_params(("arbitrary", "arbitrary")),
        name="dsa",
    )(q, qi, wt, kib, kb, vt)


def _mix_kernel(x_ref, attn_ref, u_ref, halo_ref, gate1_ref, shift2_ref, scale2_ref, g2_ref,
                wpool_ref, pscale_ref, wout_ref, x1_ref, h2_ref, xp_ref, *, tm, pos0, zero_first_halo):
    i = pl.program_id(1)
    halo = halo_ref[0]
    if zero_first_halo:
        halo = jnp.where(i == 0, 0.0, halo)
    xp_ref[0:HALO] = halo
    xp_ref[HALO:HALO + tm] = u_ref[0]
    pos = pos0 + i * tm + lax.broadcasted_iota(jnp.int32, (tm, 1), 0)
    mix = jnp.dot(attn_ref[0], wout_ref[0:ATTN_WIDTH], preferred_element_type=F32)
    for g, w in enumerate(POOL_WINDOWS):
        sl = slice(g * POOL_GROUP_DIM, (g + 1) * POOL_GROUP_DIM)
        cur = xp_ref[HALO:HALO + tm, sl]
        wsum = cur
        for dlt in range(1, w):
            wsum = wsum + xp_ref[HALO - dlt:HALO - dlt + tm, sl]
        inv_cnt = 1.0 / jnp.minimum(pos + 1, w).astype(F32)
        z = wsum * inv_cnt - cur
        y = jnp.dot(z.astype(BF16), wpool_ref[g], preferred_element_type=F32) * pscale_ref[:, sl]
        mix = mix + jnp.dot(y.astype(BF16), wout_ref[ATTN_WIDTH + g * POOL_GROUP_DIM:
                                                      ATTN_WIDTH + (g + 1) * POOL_GROUP_DIM],
                            preferred_element_type=F32)
    x1 = x_ref[0] + gate1_ref[0] * mix
    x1_ref[0] = x1
    h2_ref[0] = _rmsnorm_mod(x1, g2_ref[...], scale2_ref[0], shift2_ref[0]).astype(BF16)


def _mix(x, attn, u, halo, gate1, shift2, scale2, g2, wpool, pscale, wout, *, tm, pos0, halo_from_u):
    bsz, t, d = x.shape
    if halo_from_u:
        halo_spec = pl.BlockSpec((1, HALO, POOL_WIDTH),
                                 lambda b, i: (b, jnp.maximum(i * (tm // HALO) - 1, 0), 0))
    else:
        halo_spec = pl.BlockSpec((1, HALO, POOL_WIDTH), lambda b, i: (b, 0, 0))
    kern = functools.partial(_mix_kernel, tm=tm, pos0=pos0, zero_first_halo=halo_from_u)
    tokspec = lambda width: pl.BlockSpec((1, tm, width), lambda b, i: (b, i, 0))
    return pl.pallas_call(
        kern,
        grid=(bsz, t // tm),
        in_specs=[tokspec(d), tokspec(ATTN_WIDTH), tokspec(POOL_WIDTH), halo_spec,
                  _mod_spec(gate1, tm), _mod_spec(shift2, tm), _mod_spec(scale2, tm),
                  pl.BlockSpec((1, d), lambda b, i: (0, 0)),
                  _resident(wpool.shape, lambda b, i: (0, 0, 0)),
                  pl.BlockSpec((1, POOL_WIDTH), lambda b, i: (0, 0)),
                  _resident(wout.shape, lambda b, i: (0, 0))],
        out_specs=[tokspec(d), tokspec(d)],
        out_shape=[jax.ShapeDtypeStruct((bsz, t, d), F32), jax.ShapeDtypeStruct((bsz, t, d), BF16)],
        scratch_shapes=[pltpu.VMEM((HALO + tm, POOL_WIDTH), F32)],
        compiler_params=_params(("arbitrary", "arbitrary")),
        name="mix",
    )(x, attn, u, halo, gate1, shift2, scale2, g2.reshape(1, d), wpool, pscale.reshape(1, POOL_WIDTH), wout)


def _ffn_kernel(h_ref, x1_ref, gate2_ref, gf_ref, wg_ref, wu_ref, wd_ref, y_ref, acc_ref):
    j = pl.program_id(2)

    @pl.when(j == 0)
    def _():
        acc_ref[...] = jnp.zeros(acc_ref.shape, F32)

    h = h_ref[0]
    g = jnp.dot(h, wg_ref[...], preferred_element_type=F32)
    up = jnp.dot(h, wu_ref[...], preferred_element_type=F32)
    a = (g * jax.nn.sigmoid(g) * up).astype(BF16)
    acc_ref[...] += jnp.dot(a, wd_ref[...], preferred_element_type=F32)

    @pl.when(j == pl.num_programs(2) - 1)
    def _():
        out = x1_ref[0] + gate2_ref[0] * acc_ref[...]
        y_ref[0] = out * lax.rsqrt(jnp.mean(out * out, axis=-1, keepdims=True) + EPS) * gf_ref[...]


def _ffn(h2, x1, gate2, g_final, wg, wu, wd, *, tm, tf):
    bsz, t, d = x1.shape
    dff = wg.shape[1]
    tokspec = pl.BlockSpec((1, tm, d), lambda b, i, j: (b, i, 0))
    return pl.pallas_call(
        _ffn_kernel,
        grid=(bsz, t // tm, dff // tf),
        in_specs=[tokspec, tokspec, _mod_spec(gate2, tm),
                  pl.BlockSpec((1, d), lambda b, i, j: (0, 0)),
                  pl.BlockSpec((d, tf), lambda b, i, j: (0, j)),
                  pl.BlockSpec((d, tf), lambda b, i, j: (0, j)),
                  pl.BlockSpec((tf, d), lambda b, i, j: (j, 0))],
        out_specs=tokspec,
        out_shape=jax.ShapeDtypeStruct((bsz, t, d), F32),
        scratch_shapes=[pltpu.VMEM((tm, d), F32)],
        compiler_params=_params(("arbitrary", "arbitrary", "arbitrary")),
        name="ffn",
    )(h2, x1, gate2, g_final.reshape(1, d), wg, wu, wd)


DSA_KEY_TILE = 512


def _dsa_tiles(n_keys, topk):
    ts = max(min(DSA_KEY_TILE, n_keys), 2 * topk)
    ts = -(-ts // 256) * 256
    return ts, -(-n_keys // (ts // 2)) * (ts // 2)


def _pad_keys(a, s_pad, axis):
    pad = [(0, 0)] * a.ndim
    pad[axis] = (0, s_pad - a.shape[axis])
    return jnp.pad(a, pad)


def kernel(x_prompt, x_sample, c_prompt, c_sample, cache_k, cache_v, cache_kidx, state_pool,
           w_ada, b_ada, g_norm1, w_in, w_pool, pool_scale, w_out, g_norm2, w_gate, w_up, w_down, g_final):
    bp, tp, d = x_prompt.shape
    bs, tsmp, _ = x_sample.shape
    depth = w_ada.shape[0]
    past = cache_k.shape[2]
    assert depth == 1 and d == 2 * ATTN_WIDTH
    topk_p = min(TOPK_MAX, tp // 4)
    topk_s = min(TOPK_MAX, (past + tsmp) // 4)
    l = 0

    w = w_in[l]
    w_packed = jnp.concatenate(
        [w[:, :4096], w[:, 4176:5200], w[:, 4096:4176], jnp.zeros((d, _IN_PACKED - 5200), w.dtype)],
        axis=1).astype(BF16)
    wpool_b = w_pool[l].astype(BF16)
    wout_b = w_out[l].astype(BF16)
    wg_b, wu_b, wd_b = w_gate[l].astype(BF16), w_up[l].astype(BF16), w_down[l].astype(BF16)

    rows = bp + bs
    c_all = jnp.concatenate([c_prompt, c_sample], axis=0)
    c_all = jnp.pad(c_all, ((0, -rows % 8), (0, 0)))
    mod = _ada(c_all, w_ada[l], b_ada[l])
    mods_p = [m[:bp, None, :] for m in jnp.split(mod, 6, axis=-1)]
    mods_s = [jnp.repeat(m[bp:rows], tsmp, axis=0)[None] for m in jnp.split(mod, 6, axis=-1)]
    mods_sb = [m[bp:rows, None, :] for m in jnp.split(mod, 6, axis=-1)]

    q, k, v, qi, u, misc, kb, vt, kib, wt = _inproj(
        x_prompt, mods_p[0], mods_p[1], g_norm1[l], w_packed, tm=256)
    ts_p, sp_p = _dsa_tiles(tp, topk_p)
    assert sp_p == tp
    attn = _dsa(q, qi, wt, kib, kb, vt, pos0=0, topk=topk_p, tq=min(256, tp), ts=ts_p, tk=ts_p // 2)
    x1, h2 = _mix(x_prompt, attn, u, u, mods_p[2], mods_p[3], mods_p[4], g_norm2[l],
                  wpool_b, pool_scale[l], wout_b, tm=512, pos0=0, halo_from_u=True)
    y_prompt = _ffn(h2, x1, mods_p[5], g_final, wg_b, wu_b, wd_b, tm=512, tf=512)

    ntok = bs * tsmp
    xs_flat = x_sample.reshape(1, ntok, d)
    q_s, k_s, v_s, qi_s, u_s, misc_s, kb_s, _, kib_s, _ = _inproj(
        xs_flat, mods_s[0], mods_s[1], g_norm1[l], w_packed, tm=min(256, ntok))
    per_batch = lambda a: a.reshape(bs, tsmp, a.shape[-1])
    q_s, k_s, v_s, u_s, misc_s, kb_s, kib_s = map(per_batch, (q_s, k_s, v_s, u_s, misc_s, kb_s, kib_s))
    qi_s = jnp.swapaxes(qi_s.reshape(IDX_HEADS, bs, tsmp, IDX_DIM), 0, 1)
    s_all = past + tsmp
    ts_s, s_pad = _dsa_tiles(s_all, topk_s)
    cache_k2, cache_v2 = lax.optimization_barrier(
        (cache_k[l].reshape(bs, past, ATTN_WIDTH), cache_v[l].reshape(bs, past, ATTN_WIDTH)))
    k_all = _pad_keys(jnp.concatenate([cache_k2.astype(BF16), kb_s], axis=1), s_pad, 1)
    v_all = _pad_keys(jnp.concatenate([cache_v2.astype(BF16), v_s.astype(BF16)], axis=1), s_pad, 1)
    ki_all = _pad_keys(jnp.concatenate([cache_kidx[l].astype(BF16), kib_s], axis=1), s_pad, 1)
    wt_s = jnp.swapaxes(misc_s[..., IDX_DIM:IDX_DIM + IDX_HEADS], 1, 2)
    attn_s = _dsa(q_s, qi_s, wt_s, ki_all, k_all, _values_t(v_all),
                  pos0=past, topk=topk_s, tq=tsmp, ts=ts_s, tk=ts_s // 2)
    halo_s = jnp.pad(state_pool[l], ((0, 0), (HALO - POOL_PAD, 0), (0, 0)))
    x1_s, h2_s = _mix(x_sample, attn_s, u_s, halo_s, mods_sb[2], mods_sb[3], mods_sb[4], g_norm2[l],
                      wpool_b, pool_scale[l], wout_b, tm=tsmp, pos0=past, halo_from_u=False)
    y_sample = _ffn(h2_s.reshape(1, ntok, d), x1_s.reshape(1, ntok, d), mods_s[5], g_final,
                    wg_b, wu_b, wd_b, tm=min(256, ntok), tf=512).reshape(bs, tsmp, d)

    heads = lambda a: a.reshape(1, a.shape[0], a.shape[1], N_HEADS, HEAD_DIM)
    assert tp >= POOL_PAD and tsmp >= POOL_PAD
    pool_p = u[:, -POOL_PAD:]
    pool_s = u_s[:, -POOL_PAD:]
    return (y_prompt, y_sample,
            heads(k), heads(v), misc[..., :IDX_DIM][None], pool_p[None],
            heads(k_s), heads(v_s), misc_s[..., :IDX_DIM][None], pool_s[None])
```
